```python
import math
import jax, jax.numpy as jnp
from jax import lax
import numpy as np

D_MODEL = 2048
BATCH = 2
SEQ = 4096
DEPTH = 2
DEC_BATCH = 128
DEC_SEQ = 1
PAST_LEN = 2048
PAGE_SIZE = 128

N_MIXERS = 2
N_CONV_LAYERS = (DEPTH + 1) // 2
N_NSA_LAYERS = DEPTH // 2
CONV_W = 3
N_HEADS = 16
HEAD_DIM = D_MODEL // N_HEADS
N_KV_HEADS = 4
HPG = N_HEADS // N_KV_HEADS
KV_DIM = N_KV_HEADS * HEAD_DIM
Q_DIM = N_HEADS * HEAD_DIM
NSA_PROJ = Q_DIM + 6 * KV_DIM + 3 * N_HEADS
CMP_BLOCK = 64
CMP_HIDDEN = 256
N_SELECT = 16
WINDOW = 512
Q_BLOCK = 64
ROPE_THETA = 10000.0
PEER_HEADS = 8
PEER_KEYS = 128
PEER_EXPERTS = PEER_KEYS * PEER_KEYS
PEER_KEY_DIM = 256
PEER_TOPK = 16
PEER_CHUNK = 128
RMS_EPS = 1e-6
NEG_INF = -1e30
FORCED_SCORE = 1e6

kernel_name = 'conv_nsa_peer_hybrid_step'


def rmsnorm(x, g):
    xf = x.astype(jnp.float32)
    y = xf * lax.rsqrt(jnp.mean(xf * xf, axis=-1, keepdims=True) + RMS_EPS)
    return (y * g.astype(jnp.float32)).astype(x.dtype)


def rope(x, pos):
    half = HEAD_DIM // 2
    inv = ROPE_THETA ** (-jnp.arange(half, dtype=jnp.float32) / half)
    ang = pos.astype(jnp.float32)[:, None] * inv[None, :]
    cos = jnp.cos(ang)[None, :, None, :]
    sin = jnp.sin(ang)[None, :, None, :]
    xf = x.astype(jnp.float32)
    x1, x2 = xf[..., :half], xf[..., half:]
    return jnp.concatenate([x1 * cos - x2 * sin, x2 * cos + x1 * sin], axis=-1).astype(x.dtype)


def masked_softmax(s, mask):
    p = jax.nn.softmax(jnp.where(mask, s, NEG_INF), axis=-1)
    return jnp.where(mask, p, 0.0)


def short_conv_mixer(x, conv_buf, w_in, conv_w, w_out):
    t = x.shape[1]
    b_gate, c_gate, h = jnp.split(x @ w_in, 3, axis=-1)
    u = c_gate * h
    u_ext = jnp.concatenate([conv_buf.astype(u.dtype), u], axis=1)
    z = u_ext[:, 0:t] * conv_w[0] + u_ext[:, 1:t + 1] * conv_w[1] + u_ext[:, 2:t + 2] * conv_w[2]
    return (b_gate * z) @ w_out, u_ext[:, -(CONV_W - 1):]


def nsa_project(x, w_in, pos):
    b, t, _ = x.shape
    p = x @ w_in
    q = rope(p[..., :Q_DIM].reshape(b, t, N_HEADS, HEAD_DIM), pos)
    kv = p[..., Q_DIM:Q_DIM + 6 * KV_DIM].reshape(b, t, 6, N_KV_HEADS, HEAD_DIM)
    gates = p[..., Q_DIM + 6 * KV_DIM:].reshape(b, t, N_HEADS, 3)
    k_c = rope(kv[:, :, 0], pos)
    v_c = kv[:, :, 1]
    k_s = rope(kv[:, :, 2], pos)
    v_s = kv[:, :, 3]
    k_w = rope(kv[:, :, 4], pos)
    v_w = kv[:, :, 5]
    return q, k_c, v_c, k_s, v_s, k_w, v_w, gates


def nsa_compress(k, pe, w1, w2):
    b, t = k.shape[:2]
    nb = t // CMP_BLOCK
    kb = k.reshape(b, nb, CMP_BLOCK, N_KV_HEADS, HEAD_DIM) + pe[None, None, :, None, :]
    kb = jnp.transpose(kb, (0, 1, 3, 2, 4)).reshape(b, nb, N_KV_HEADS, CMP_BLOCK * HEAD_DIM)
    return jax.nn.gelu(kb @ w1) @ w2


def nsa_attend(q, gates, q_pos, kc, vc, gather_sel, kw, vw, kw_pos):
    b, tq = q.shape[:2]
    nb = kc.shape[1]
    scale = HEAD_DIM ** -0.5
    qg = q.reshape(b, tq, N_KV_HEADS, HPG, HEAD_DIM)
    blk = jnp.arange(nb)
    c_mask = ((blk + 1) * CMP_BLOCK - 1)[None, :] <= q_pos[:, None]
    s_c = jnp.einsum('bqghd,bngd->bqghn', qg, kc).astype(jnp.float32) * scale
    p_c = masked_softmax(s_c, c_mask[None, :, None, None, :])
    o_c = jnp.einsum('bqghn,bngd->bqghd', p_c.astype(vc.dtype), vc)
    cur = q_pos // CMP_BLOCK
    reach = blk[None, :] <= cur[:, None]
    forced = reach & ((blk[None, :] == 0) | (blk[None, :] >= cur[:, None] - 1))
    imp = p_c.sum(axis=3)
    score = jnp.where(forced[None, :, None, :], FORCED_SCORE,
                      jnp.where(reach[None, :, None, :], imp, NEG_INF))
    n_sel = min(N_SELECT, nb)
    _, idx = lax.top_k(score, n_sel)
    ks, vs = gather_sel(idx)
    tok_pos = idx[..., None] * CMP_BLOCK + jnp.arange(CMP_BLOCK)
    s_mask = (tok_pos <= q_pos[None, :, None, None, None]).reshape(b, tq, N_KV_HEADS, 1, n_sel * CMP_BLOCK)
    s_s = jnp.einsum('bqghd,bqgkld->bqghkl', qg, ks).astype(jnp.float32) * scale
    p_s = masked_softmax(s_s.reshape(b, tq, N_KV_HEADS, HPG, n_sel * CMP_BLOCK), s_mask)
    p_s = p_s.reshape(b, tq, N_KV_HEADS, HPG, n_sel, CMP_BLOCK)
    o_s = jnp.einsum('bqghkl,bqgkld->bqghd', p_s.astype(vs.dtype), vs)
    dist = q_pos[:, None] - kw_pos[None, :]
    w_mask = (dist >= 0) & (dist <= WINDOW) & (kw_pos[None, :] >= 0)
    s_w = jnp.einsum('bqghd,bkgd->bqghk', qg, kw).astype(jnp.float32) * scale
    p_w = masked_softmax(s_w, w_mask[None, :, None, None, :])
    o_w = jnp.einsum('bqghk,bkgd->bqghd', p_w.astype(vw.dtype), vw)
    g = jax.nn.sigmoid(gates.astype(jnp.float32)).reshape(b, tq, N_KV_HEADS, HPG, 3).astype(q.dtype)
    o = g[..., 0:1] * o_c + g[..., 1:2] * o_s + g[..., 2:3] * o_w
    return o.reshape(b, tq, N_HEADS, HEAD_DIM)


def nsa_prompt(x, w_in, w_out, pe_k, w1_k, w2_k, pe_v, w1_v, w2_v):
    b, t, _ = x.shape
    pos = jnp.arange(t)
    q, k_c, v_c, k_s, v_s, k_w, v_w, gates = nsa_project(x, w_in, pos)
    kc = nsa_compress(k_c, pe_k, w1_k, w2_k)
    vc = nsa_compress(v_c, pe_v, w1_v, w2_v)
    nb = t // CMP_BLOCK
    ks_blk = jnp.transpose(k_s.reshape(b, nb, CMP_BLOCK, N_KV_HEADS, HEAD_DIM), (0, 3, 1, 2, 4))
    vs_blk = jnp.transpose(v_s.reshape(b, nb, CMP_BLOCK, N_KV_HEADS, HEAD_DIM), (0, 3, 1, 2, 4))
    b_ix = jnp.arange(b)[:, None, None, None]
    g_ix = jnp.arange(N_KV_HEADS)[None, None, :, None]

    def gather_sel(idx):
        return ks_blk[b_ix, g_ix, idx], vs_blk[b_ix, g_ix, idx]

    kw_pad = jnp.pad(k_w, ((0, 0), (WINDOW, 0), (0, 0), (0, 0)))
    vw_pad = jnp.pad(v_w, ((0, 0), (WINDOW, 0), (0, 0), (0, 0)))
    n_ch = t // Q_BLOCK
    q_ch = jnp.transpose(q.reshape(b, n_ch, Q_BLOCK, N_HEADS, HEAD_DIM), (1, 0, 2, 3, 4))
    g_ch = jnp.transpose(gates.reshape(b, n_ch, Q_BLOCK, N_HEADS, 3), (1, 0, 2, 3, 4))

    def one(args):
        c, qc, gc = args
        start = c * Q_BLOCK
        q_pos = start + jnp.arange(Q_BLOCK)
        kw = lax.dynamic_slice_in_dim(kw_pad, start, WINDOW + Q_BLOCK, axis=1)
        vw = lax.dynamic_slice_in_dim(vw_pad, start, WINDOW + Q_BLOCK, axis=1)
        kw_pos = start - WINDOW + jnp.arange(WINDOW + Q_BLOCK)
        return nsa_attend(qc, gc, q_pos, kc, vc, gather_sel, kw, vw, kw_pos)

    o = lax.map(one, (jnp.arange(n_ch), q_ch, g_ch))
    o = jnp.transpose(o, (1, 0, 2, 3, 4)).reshape(b, t, Q_DIM)
    rows = jnp.transpose(jnp.stack([k_c, v_c, k_s, v_s], axis=1), (0, 1, 3, 2, 4))
    wb = min(WINDOW, t)
    win = jnp.transpose(jnp.stack([k_w[:, -wb:], v_w[:, -wb:]], axis=1), (0, 1, 3, 2, 4))
    return o @ w_out, rows, win


def nsa_sample(x, kv_pool, win_buf, page_table, layer, w_in, w_out, pe_k, w1_k, w2_k, pe_v, w1_v, w2_v):
    b, t, _ = x.shape
    n_pages = page_table.shape[1]
    past = n_pages * PAGE_SIZE
    pos = past + jnp.arange(t)
    q, k_c, v_c, k_s, v_s, k_w, v_w, gates = nsa_project(x, w_in, pos)
    tot = past + t
    nb = -(-tot // CMP_BLOCK)
    pad = nb * CMP_BLOCK - tot

    def full_rows(comp, new):
        r = kv_pool[page_table, layer, comp]
        r = jnp.transpose(r, (0, 1, 3, 2, 4)).reshape(b, past, N_KV_HEADS, HEAD_DIM)
        return jnp.pad(jnp.concatenate([r.astype(new.dtype), new], axis=1), ((0, 0), (0, pad), (0, 0), (0, 0)))

    kc = nsa_compress(full_rows(0, k_c), pe_k, w1_k, w2_k)
    vc = nsa_compress(full_rows(1, v_c), pe_v, w1_v, w2_v)
    bpp = PAGE_SIZE // CMP_BLOCK
    nb_past = past // CMP_BLOCK
    nb_new = nb - nb_past
    pool_blk = kv_pool.reshape(kv_pool.shape[:4] + (bpp, CMP_BLOCK, HEAD_DIM))

    def new_blocks(r):
        r = jnp.pad(r, ((0, 0), (0, pad), (0, 0), (0, 0)))
        return jnp.transpose(r.reshape(b, nb_new, CMP_BLOCK, N_KV_HEADS, HEAD_DIM), (0, 3, 1, 2, 4))

    ks_new, vs_new = new_blocks(k_s), new_blocks(v_s)
    b_ix = jnp.arange(b)[:, None, None, None]
    g_ix = jnp.arange(N_KV_HEADS)[None, None, :, None]

    def gather_sel(idx):
        in_past = (idx < nb_past)[..., None, None]
        jp = jnp.minimum(idx, nb_past - 1)
        page = page_table[b_ix, jp // bpp]
        sub = jp % bpp
        jn = jnp.clip(idx - nb_past, 0, nb_new - 1)
        k = jnp.where(in_past, pool_blk[page, layer, 2, g_ix, sub].astype(ks_new.dtype), ks_new[b_ix, g_ix, jn])
        v = jnp.where(in_past, pool_blk[page, layer, 3, g_ix, sub].astype(vs_new.dtype), vs_new[b_ix, g_ix, jn])
        return k, v

    wb = win_buf.shape[4]
    kw = jnp.concatenate([jnp.transpose(win_buf[:, layer, 0], (0, 2, 1, 3)).astype(k_w.dtype), k_w], axis=1)
    vw = jnp.concatenate([jnp.transpose(win_buf[:, layer, 1], (0, 2, 1, 3)).astype(v_w.dtype), v_w], axis=1)
    kw_pos = past - wb + jnp.arange(wb + t)
    o = nsa_attend(q, gates, pos, kc, vc, gather_sel, kw, vw, kw_pos).reshape(b, t, Q_DIM)
    rows = jnp.transpose(jnp.stack([k_c, v_c, k_s, v_s], axis=1), (0, 1, 3, 2, 4))
    win = jnp.transpose(jnp.stack([kw[:, -wb:], vw[:, -wb:]], axis=1), (0, 1, 3, 2, 4))
    return o @ w_out, rows, win


def peer_ffn(x, w_q, sub_keys, u, v):
    n = x.shape[0]
    c = min(PEER_CHUNK, n)
    n_ch = -(-n // c)
    xs = jnp.pad(x, ((0, n_ch * c - n), (0, 0))).reshape(n_ch, c, D_MODEL)

    def one(xc):
        qh = (xc @ w_q).reshape(c, PEER_HEADS, 2, PEER_KEY_DIM // 2)
        s = jnp.einsum('chpk,hpnk->chpn', qh, sub_keys).astype(jnp.float32)
        s_half, i_half = lax.top_k(s, PEER_TOPK)
        cand = (s_half[:, :, 0, :, None] + s_half[:, :, 1, None, :]).reshape(c, PEER_HEADS, PEER_TOPK * PEER_TOPK)
        cand_idx = (i_half[:, :, 0, :, None] * PEER_KEYS + i_half[:, :, 1, None, :]).reshape(c, PEER_HEADS, PEER_TOPK * PEER_TOPK)
        top_s, top_i = lax.top_k(cand, PEER_TOPK)
        eidx = jnp.take_along_axis(cand_idx, top_i, axis=-1)
        gate = jax.nn.softmax(top_s, axis=-1)
        act = jax.nn.gelu(jnp.einsum('chkd,cd->chk', u[eidx], xc).astype(jnp.float32))
        return jnp.einsum('chk,chkd->cd', (gate * act).astype(v.dtype), v[eidx])

    out = lax.map(one, xs).reshape(n_ch * c, D_MODEL)
    return out[:n]


def peer_apply(h, w_q, sub_keys, u, v):
    return peer_ffn(h.reshape(-1, D_MODEL), w_q, sub_keys, u, v).reshape(h.shape).astype(h.dtype)


def setup_inputs(seed: int = 0) -> dict:
    key = jax.random.key(seed)
    ks = jax.random.split(key, 24)
    f32 = jnp.float32
    n_pages = PAST_LEN // PAGE_SIZE
    n_used = DEC_BATCH * n_pages
    n_phys = (5 * n_used + 3) // 4
    win_len = min(WINDOW, PAST_LEN)

    def nrm(k, shape, scale):
        return jax.random.normal(k, shape, f32) * scale

    return {
        'x_prompt': nrm(ks[0], (BATCH, SEQ, D_MODEL), 1.0),
        'x_sample': nrm(ks[1], (DEC_BATCH, DEC_SEQ, D_MODEL), 1.0),
        'state_conv': nrm(ks[2], (DEC_BATCH, N_CONV_LAYERS, CONV_W - 1, D_MODEL), 1.0),
        'cache_nsa_kv': nrm(ks[3], (n_phys, N_NSA_LAYERS, 4, N_KV_HEADS, PAGE_SIZE, HEAD_DIM), 1.0),
        'state_nsa_win': nrm(ks[4], (DEC_BATCH, N_NSA_LAYERS, 2, N_KV_HEADS, win_len, HEAD_DIM), 1.0),
        'page_table': jax.random.permutation(ks[5], n_phys)[:n_used].reshape(DEC_BATCH, n_pages).astype(jnp.int32),
        'norm_mix': 1.0 + nrm(ks[6], (DEPTH, D_MODEL), 0.02),
        'norm_ffn': 1.0 + nrm(ks[7], (DEPTH, D_MODEL), 0.02),
        'norm_final': 1.0 + nrm(ks[8], (D_MODEL,), 0.02),
        'conv_w_in': nrm(ks[9], (N_CONV_LAYERS, D_MODEL, 3 * D_MODEL), D_MODEL ** -0.5),
        'conv_w': nrm(ks[10], (N_CONV_LAYERS, CONV_W, D_MODEL), CONV_W ** -0.5),
        'conv_w_out': nrm(ks[11], (N_CONV_LAYERS, D_MODEL, D_MODEL), D_MODEL ** -0.5),
        'nsa_w_in': nrm(ks[12], (N_NSA_LAYERS, D_MODEL, NSA_PROJ), D_MODEL ** -0.5),
        'nsa_w_out': nrm(ks[13], (N_NSA_LAYERS, Q_DIM, D_MODEL), Q_DIM ** -0.5),
        'nsa_cmp_pe_k': nrm(ks[14], (N_NSA_LAYERS, CMP_BLOCK, HEAD_DIM), 0.1),
        'nsa_cmp_w1_k': nrm(ks[15], (N_NSA_LAYERS, CMP_BLOCK * HEAD_DIM, CMP_HIDDEN), (CMP_BLOCK * HEAD_DIM) ** -0.5),
        'nsa_cmp_w2_k': nrm(ks[16], (N_NSA_LAYERS, CMP_HIDDEN, HEAD_DIM), CMP_HIDDEN ** -0.5),
        'nsa_cmp_pe_v': nrm(ks[17], (N_NSA_LAYERS, CMP_BLOCK, HEAD_DIM), 0.1),
        'nsa_cmp_w1_v': nrm(ks[18], (N_NSA_LAYERS, CMP_BLOCK * HEAD_DIM, CMP_HIDDEN), (CMP_BLOCK * HEAD_DIM) ** -0.5),
        'nsa_cmp_w2_v': nrm(ks[19], (N_NSA_LAYERS, CMP_HIDDEN, HEAD_DIM), CMP_HIDDEN ** -0.5),
        'peer_w_q': nrm(ks[20], (DEPTH, D_MODEL, PEER_HEADS * PEER_KEY_DIM), D_MODEL ** -0.5),
        'peer_sub_keys': nrm(ks[21], (DEPTH, PEER_HEADS, 2, PEER_KEYS, PEER_KEY_DIM // 2), (PEER_KEY_DIM // 2) ** -0.5),
        'peer_u': nrm(ks[22], (DEPTH, PEER_EXPERTS, D_MODEL), D_MODEL ** -0.5),
        'peer_v': nrm(ks[23], (DEPTH, PEER_EXPERTS, D_MODEL), PEER_HEADS ** -0.5),
    }


def reference(x_prompt, x_sample, state_conv, cache_nsa_kv, state_nsa_win, page_table,
              norm_mix, norm_ffn, norm_final, conv_w_in, conv_w, conv_w_out,
              nsa_w_in, nsa_w_out, nsa_cmp_pe_k, nsa_cmp_w1_k, nsa_cmp_w2_k,
              nsa_cmp_pe_v, nsa_cmp_w1_v, nsa_cmp_w2_v,
              peer_w_q, peer_sub_keys, peer_u, peer_v):
    xp, xs = x_prompt, x_sample
    conv_p, conv_s, rows_p, rows_s, win_p, win_s = [], [], [], [], [], []
    for i in range(DEPTH):
        hp = rmsnorm(xp, norm_mix[i])
        hs = rmsnorm(xs, norm_mix[i])
        if i % N_MIXERS == 0:
            a = i // N_MIXERS
            zero_buf = jnp.zeros((xp.shape[0], CONV_W - 1, D_MODEL), xp.dtype)
            yp, bp = short_conv_mixer(hp, zero_buf, conv_w_in[a], conv_w[a], conv_w_out[a])
            ys, bs = short_conv_mixer(hs, state_conv[:, a], conv_w_in[a], conv_w[a], conv_w_out[a])
            conv_p.append(bp)
            conv_s.append(bs)
        else:
            n = i // N_MIXERS
            yp, rp, wp = nsa_prompt(hp, nsa_w_in[n], nsa_w_out[n], nsa_cmp_pe_k[n], nsa_cmp_w1_k[n], nsa_cmp_w2_k[n],
                                    nsa_cmp_pe_v[n], nsa_cmp_w1_v[n], nsa_cmp_w2_v[n])
            ys, rs, ws = nsa_sample(hs, cache_nsa_kv, state_nsa_win, page_table, n, nsa_w_in[n], nsa_w_out[n],
                                    nsa_cmp_pe_k[n], nsa_cmp_w1_k[n], nsa_cmp_w2_k[n],
                                    nsa_cmp_pe_v[n], nsa_cmp_w1_v[n], nsa_cmp_w2_v[n])
            rows_p.append(rp)
            rows_s.append(rs)
            win_p.append(wp)
            win_s.append(ws)
        xp = xp + yp
        xs = xs + ys
        xp = xp + peer_apply(rmsnorm(xp, norm_ffn[i]), peer_w_q[i], peer_sub_keys[i], peer_u[i], peer_v[i])
        xs = xs + peer_apply(rmsnorm(xs, norm_ffn[i]), peer_w_q[i], peer_sub_keys[i], peer_u[i], peer_v[i])
    y_prompt = rmsnorm(xp, norm_final)
    y_sample = rmsnorm(xs, norm_final)
    conv_state_prompt = jnp.stack(conv_p, axis=1)
    conv_state_sample = jnp.stack(conv_s, axis=1)
    nsa_rows_prompt = jnp.stack(rows_p, axis=1)
    nsa_rows_sample = jnp.stack(rows_s, axis=1)
    nsa_win_prompt = jnp.stack(win_p, axis=1)
    nsa_win_sample = jnp.stack(win_s, axis=1)
    return (y_prompt, y_sample, conv_state_prompt, conv_state_sample, nsa_rows_prompt, nsa_rows_sample, nsa_win_prompt, nsa_win_sample)
```

```python
import functools
import math

import jax
import jax.numpy as jnp
from jax import lax
from jax.experimental import pallas as pl
from jax.experimental.pallas import tpu as pltpu

F32 = jnp.float32
BF16 = jnp.bfloat16

D_MODEL = 2048
RMS_EPS = 1e-6

PEER_HEADS = 8
PEER_KEYS = 128
PEER_TOPK = 16
PEER_HALF_DIM = 128
PEER_EXPERTS = PEER_KEYS * PEER_KEYS

VMEM_LIMIT_BIG = 60 * 1024 * 1024
VMEM_LIMIT_MID = 48 * 1024 * 1024

NT_DIMS = (((1,), (1,)), ((), ()))


def _gelu_tanh(x):
    c = math.sqrt(2.0 / math.pi)
    return 0.5 * x * (1.0 + jnp.tanh(c * (x + 0.044715 * (x * x * x))))


def _rms_matmul_kernel(x_ref, g_ref, w_ref, o_ref, *rest, emit_ht):
    if emit_ht:
        ht_ref, h_scr = rest
    else:
        (h_scr,) = rest

    @pl.when(pl.program_id(1) == 0)
    def _():
        x = x_ref[...]
        ms = jnp.mean(x * x, axis=-1, keepdims=True)
        h = x * lax.rsqrt(ms + RMS_EPS) * g_ref[...]
        h_scr[...] = h.astype(BF16)
        if emit_ht:
            ht_ref[...] = h.T.astype(BF16)

    o_ref[...] = jnp.dot(h_scr[...], w_ref[...], preferred_element_type=F32)


def rms_matmul(x, g, w, *, tm, tn, emit_ht=False):
    n, d = x.shape
    m = w.shape[1]
    assert n % tm == 0 and m % tn == 0
    out_shape = [jax.ShapeDtypeStruct((n, m), F32)]
    out_specs = [pl.BlockSpec((tm, tn), lambda i, j: (i, j))]
    if emit_ht:
        out_shape.append(jax.ShapeDtypeStruct((d, n), BF16))
        out_specs.append(pl.BlockSpec((d, tm), lambda i, j: (0, i)))
    res = pl.pallas_call(
        functools.partial(_rms_matmul_kernel, emit_ht=emit_ht),
        grid=(n // tm, m // tn),
        in_specs=[
            pl.BlockSpec((tm, d), lambda i, j: (i, 0)),
            pl.BlockSpec((1, d), lambda i, j: (0, 0)),
            pl.BlockSpec((d, tn), lambda i, j: (0, j)),
        ],
        out_specs=out_specs,
        out_shape=out_shape,
        scratch_shapes=[pltpu.VMEM((tm, d), BF16)],
        compiler_params=pltpu.CompilerParams(
            dimension_semantics=("parallel", "arbitrary"), vmem_limit_bytes=VMEM_LIMIT_MID),
        name="rms_matmul_ht" if emit_ht else "rms_matmul",
    )(x, g.reshape(1, d), w)
    return res if emit_ht else res[0]


NEG_BIG = -jnp.inf
NOT_RANKED = 99.0


def _top16_rows(s):
    rows = lax.broadcasted_iota(jnp.int32, s.shape, 0).astype(F32)
    rank = jnp.full(s.shape, NOT_RANKED, F32)
    vals = []
    for k in range(PEER_TOPK):
        m = jnp.max(s, axis=0, keepdims=True)
        idx = jnp.min(jnp.where(s == m, rows, float(PEER_KEYS)), axis=0, keepdims=True)
        hit = rows == idx
        rank = jnp.where(hit, float(k), rank)
        s = jnp.where(hit, NEG_BIG, s)
        vals.append(m)
    return vals, rank


def _pair_counts(a, b):
    t = a[0].shape[1]
    b16 = jnp.concatenate(b, axis=0)
    a_hi = jnp.concatenate(a[8:], axis=0)
    sub8 = lax.broadcasted_iota(jnp.int32, (8, t), 0).astype(F32)
    sub16 = lax.broadcasted_iota(jnp.int32, (16, t), 0).astype(F32)
    cands = [a[0] + b16]
    flats = [sub16]
    for k0 in range(1, 8):
        lim = PEER_TOPK // (k0 + 1)
        c = a[k0] + b16[:8]
        if lim < 8:
            c = jnp.where(sub8 < float(lim), c, NEG_BIG)
        cands.append(c)
        flats.append(sub8 + float(k0 * 16))
    cands.append(a_hi + b[0])
    flats.append((sub8 + 8.0) * 16.0)

    cmax = a[0] + b[0]
    counts = [jnp.zeros(c.shape, F32) for c in cands]
    z = jnp.zeros((1, t), F32)
    for _ in range(PEER_TOPK):
        m = cands[0].max(axis=0, keepdims=True)
        for c in cands[1:]:
            m = jnp.maximum(m, c.max(axis=0, keepdims=True))
        idx = None
        for c, f in zip(cands, flats):
            i = jnp.min(jnp.where(c == m, f, 999.0), axis=0, keepdims=True)
            idx = i if idx is None else jnp.minimum(idx, i)
        for j, f in enumerate(flats):
            hit = f == idx
            counts[j] = counts[j] + hit.astype(F32)
            cands[j] = jnp.where(hit, NEG_BIG, cands[j])
        z = z + jnp.exp(m - cmax)
    n = [c.sum(axis=0, keepdims=True) for c in counts[:8]]
    n += [counts[8][r:r + 1] for r in range(8)]
    return n, z


def _peer_route_kernel(q_ref, keys_ref, r1_ref, cnt_ref, a_ref, b_ref):
    def head(h, carry):
        col = pl.multiple_of(h * (2 * PEER_HALF_DIM), 2 * PEER_HALF_DIM)
        q0 = q_ref[:, pl.ds(col, PEER_HALF_DIM)].astype(BF16)
        q1 = q_ref[:, pl.ds(col + PEER_HALF_DIM, PEER_HALF_DIM)].astype(BF16)
        s0 = lax.dot_general(keys_ref[2 * h], q0, NT_DIMS, preferred_element_type=F32)
        s1 = lax.dot_general(keys_ref[2 * h + 1], q1, NT_DIMS, preferred_element_type=F32)
        v0, rank0 = _top16_rows(s0)
        v1, rank1 = _top16_rows(s1)
        n, z = _pair_counts(v0, v1)
        cnt = jnp.zeros(s0.shape, F32)
        for k0 in range(PEER_TOPK):
            cnt = jnp.where(rank0 == float(k0), n[k0], cnt)
        r1_ref[h] = rank1
        cnt_ref[h] = cnt
        a_ref[h] = jnp.exp(s0 - v0[0])
        b_ref[h] = jnp.exp(s1 - v1[0]) / z
        return carry

    lax.fori_loop(0, PEER_HEADS, head, 0)


def peer_route(q, keys, *, tm):
    n = q.shape[0]
    assert n % tm == 0
    sds = jax.ShapeDtypeStruct((PEER_HEADS, PEER_KEYS, n), F32)
    ospec = pl.BlockSpec((PEER_HEADS, PEER_KEYS, tm), lambda i: (0, 0, i))
    return pl.pallas_call(
        _peer_route_kernel,
        grid=(n // tm,),
        in_specs=[
            pl.BlockSpec((tm, PEER_HEADS * 2 * PEER_HALF_DIM), lambda i: (i, 0)),
            pl.BlockSpec((2 * PEER_HEADS, PEER_KEYS, PEER_HALF_DIM), lambda i: (0, 0, 0)),
        ],
        out_specs=[ospec] * 4,
        out_shape=[sds] * 4,
        compiler_params=pltpu.CompilerParams(
            dimension_semantics=("parallel",), vmem_limit_bytes=VMEM_LIMIT_MID),
        name="peer_route",
    )(q, keys)


PEER_TE = 1024
PEER_ROWS = PEER_TE // PEER_KEYS


def _peer_main_kernel(ht_ref, r1_ref, b_ref, cnt_ref, a_ref, u_ref, vt_ref, x_ref, o_ref,
                      acc_ref, act_ref, p_ref, *, tm):
    e = pl.program_id(1)

    @pl.when(e == 0)
    def _():
        acc_ref[...] = jnp.zeros_like(acc_ref)

    act_ref[...] = jnp.dot(u_ref[...], ht_ref[...], preferred_element_type=F32)

    n_tc = tm // 128

    def block(r, tc, carry):
        rows = pl.ds(r * PEER_KEYS, PEER_KEYS)
        cols = pl.ds(pl.multiple_of(tc * 128, 128), 128)
        g = _gelu_tanh(act_ref[rows, cols])
        w = jnp.zeros((PEER_KEYS, 128), F32)
        for h in range(PEER_HEADS):
            cnt = cnt_ref[h, 0, r:r + 1, cols]
            a = a_ref[h, 0, r:r + 1, cols]
            sel = r1_ref[h, :, cols] < cnt
            w = w + jnp.where(sel, a * b_ref[h, :, cols], 0.0)
        p_ref[rows, cols] = (w * g).astype(BF16)
        return carry

    for r in range(PEER_ROWS):
        lax.fori_loop(0, n_tc, functools.partial(block, r), 0)

    acc_ref[...] += jnp.dot(vt_ref[...], p_ref[...], preferred_element_type=F32)

    @pl.when(e == pl.num_programs(1) - 1)
    def _():
        o_ref[...] = x_ref[...] + acc_ref[...].T


def peer_main(ht, route, u, vt, x, *, tm):
    d, n = ht.shape
    rank1, cnt0, a, b = route
    n_e = PEER_EXPERTS // PEER_TE
    rows_shape = (PEER_HEADS, n_e, PEER_ROWS, n)
    full_spec = pl.BlockSpec((PEER_HEADS, PEER_KEYS, tm), lambda i, e: (0, 0, i))
    rows_spec = pl.BlockSpec((PEER_HEADS, 1, PEER_ROWS, tm), lambda i, e: (0, e, 0, i))
    return pl.pallas_call(
        functools.partial(_peer_main_kernel, tm=tm),
        grid=(n // tm, n_e),
        in_specs=[
            pl.BlockSpec((d, tm), lambda i, e: (0, i)),
            full_spec, full_spec, rows_spec, rows_spec,
            pl.BlockSpec((PEER_TE, d), lambda i, e: (e, 0)),
            pl.BlockSpec((d, PEER_TE), lambda i, e: (0, e)),
            pl.BlockSpec((tm, d), lambda i, e: (i, 0)),
        ],
        out_specs=pl.BlockSpec((tm, d), lambda i, e: (i, 0)),
        out_shape=jax.ShapeDtypeStruct((n, d), F32),
        scratch_shapes=[
            pltpu.VMEM((d, tm), F32),
            pltpu.VMEM((PEER_TE, tm), F32),
            pltpu.VMEM((PEER_TE, tm), BF16),
        ],
        compiler_params=pltpu.CompilerParams(
            dimension_semantics=("parallel", "arbitrary"), vmem_limit_bytes=VMEM_LIMIT_BIG),
        name="peer_main",
    )(ht, rank1, b, cnt0.reshape(rows_shape), a.reshape(rows_shape), u, vt, x)


def peer_layer(x, g, w_q, keys, u, vt, *, tm_q=512, tm_route=256, tm_main=512):
    q, ht = rms_matmul(x, g, w_q, tm=tm_q, tn=1024, emit_ht=True)
    route = peer_route(q, keys, tm=tm_route)
    return peer_main(ht, route, u, vt, x, tm=tm_main)


CONV_TM = 256
CONV_TAIL = 8


def _conv_core_kernel(b_ref, c_ref, h_ref, x_ref, s0_ref, s1_ref, cw_ref, wout_ref,
                      o_ref, tail_ref, us_ref, ubuf, *, prompt_tiles, tiles_per_seq):
    i = pl.program_id(0)
    tm = CONV_TM
    u = c_ref[...] * h_ref[...]
    w0 = cw_ref[0:1, :]
    w1 = cw_ref[1:2, :]
    w2 = cw_ref[2:3, :]

    @pl.when(i % tiles_per_seq == 0)
    def _():
        ubuf[0:CONV_TAIL, :] = jnp.zeros((CONV_TAIL, D_MODEL), F32)

    ubuf[CONV_TAIL:CONV_TAIL + tm, :] = u
    tail_ref[0] = u[tm - CONV_TAIL:, :]

    @pl.when(i < prompt_tiles)
    def _():
        z = ubuf[CONV_TAIL - 2:CONV_TAIL - 2 + tm, :] * w0 + ubuf[CONV_TAIL - 1:CONV_TAIL - 1 + tm, :] * w1 + u * w2
        y = jnp.dot((b_ref[...] * z).astype(BF16), wout_ref[...], preferred_element_type=F32)
        o_ref[...] = x_ref[...] + y
        ubuf[0:CONV_TAIL, :] = u[tm - CONV_TAIL:, :]

    @pl.when(i >= prompt_tiles)
    def _():
        z = s0_ref[...] * w0 + s1_ref[...] * w1 + u * w2
        y = jnp.dot((b_ref[...] * z).astype(BF16), wout_ref[...], preferred_element_type=F32)
        o_ref[...] = x_ref[...] + y
        us_ref[...] = u


def conv_core(p, x, s0, s1, cw, wout, *, n_prompt, seq_len):
    n, d = x.shape
    tm = CONV_TM
    n_tiles = n // tm
    prompt_tiles = n_prompt // tm
    tiles_per_seq = seq_len // tm
    n_dec = n - n_prompt
    dec_map = lambda i: (jnp.maximum(i - prompt_tiles, 0), 0)
    return pl.pallas_call(
        functools.partial(_conv_core_kernel, prompt_tiles=prompt_tiles, tiles_per_seq=tiles_per_seq),
        grid=(n_tiles,),
        in_specs=[
            pl.BlockSpec((tm, d), lambda i: (i, 0)),
            pl.BlockSpec((tm, d), lambda i: (i, 1)),
            pl.BlockSpec((tm, d), lambda i: (i, 2)),
            pl.BlockSpec((tm, d), lambda i: (i, 0)),
            pl.BlockSpec((tm, d), dec_map),
            pl.BlockSpec((tm, d), dec_map),
            pl.BlockSpec((3, d), lambda i: (0, 0)),
            pl.BlockSpec((d, d), lambda i: (0, 0)),
        ],
        out_specs=[
            pl.BlockSpec((tm, d), lambda i: (i, 0)),
            pl.BlockSpec((1, CONV_TAIL, d), lambda i: (i, 0, 0)),
            pl.BlockSpec((tm, d), dec_map),
        ],
        out_shape=[
            jax.ShapeDtypeStruct((n, d), F32),
            jax.ShapeDtypeStruct((n_tiles, CONV_TAIL, d), F32),
            jax.ShapeDtypeStruct((n_dec, d), F32),
        ],
        scratch_shapes=[pltpu.VMEM((CONV_TAIL + tm, d), F32)],
        compiler_params=pltpu.CompilerParams(
            dimension_semantics=("arbitrary",), vmem_limit_bytes=VMEM_LIMIT_BIG),
        name="conv_core",
    )(p, p, p, x, s0, s1, cw, wout)


N_HEADS = 16
N_KV_HEADS = 4
HPG = N_HEADS // N_KV_HEADS
HEAD_DIM = 128
Q_DIM = N_HEADS * HEAD_DIM
KV_DIM = N_KV_HEADS * HEAD_DIM
CMP_BLOCK = 64
N_SELECT = 16
WINDOW = 512
ROPE_THETA = 10000.0
NEG_INF = -1e30
FORCED_SCORE = 1e6


def _rope(x, cos, sin_signed):
    return x * cos + pltpu.roll(x, HEAD_DIM // 2, 1) * sin_signed


def _nsa_layout_kernel(p_ref, cos_ref, sin_ref, q_ref, rows_ref, win_ref):
    cos = cos_ref[...]
    sin = sin_ref[...]
    scale = HEAD_DIM ** -0.5
    for h in range(N_HEADS):
        q_ref[0, h] = (_rope(p_ref[:, h * HEAD_DIM:(h + 1) * HEAD_DIM], cos, sin) * scale).astype(BF16)
    for c in range(6):
        for g in range(N_KV_HEADS):
            col = Q_DIM + c * KV_DIM + g * HEAD_DIM
            v = p_ref[:, col:col + HEAD_DIM]
            if c % 2 == 0:
                v = _rope(v, cos, sin)
            if c < 4:
                rows_ref[0, 0, c, g] = v
            else:
                win_ref[0, c - 4, g] = v


def nsa_layout(p, cos, sin, *, n_seq, seq_len, row0, tq):
    width = Q_DIM + 6 * KV_DIM
    n_t = seq_len // tq
    blk0 = row0 // tq
    assert row0 % tq == 0
    return pl.pallas_call(
        _nsa_layout_kernel,
        grid=(n_seq, n_t),
        in_specs=[
            pl.BlockSpec((tq, width), lambda b, t: (blk0 + b * n_t + t, 0)),
            pl.BlockSpec((tq, HEAD_DIM), lambda b, t: (t, 0)),
            pl.BlockSpec((tq, HEAD_DIM), lambda b, t: (t, 0)),
        ],
        out_specs=[
            pl.BlockSpec((1, N_HEADS, tq, HEAD_DIM), lambda b, t: (b, 0, t, 0)),
            pl.BlockSpec((1, 1, 4, N_KV_HEADS, tq, HEAD_DIM), lambda b, t: (b, 0, 0, 0, t, 0)),
            pl.BlockSpec((1, 2, N_KV_HEADS, tq, HEAD_DIM), lambda b, t: (b, 0, 0, t, 0)),
        ],
        out_shape=[
            jax.ShapeDtypeStruct((n_seq, N_HEADS, seq_len, HEAD_DIM), BF16),
            jax.ShapeDtypeStruct((n_seq, 1, 4, N_KV_HEADS, seq_len, HEAD_DIM), F32),
            jax.ShapeDtypeStruct((n_seq, 2, N_KV_HEADS, seq_len, HEAD_DIM), F32),
        ],
        compiler_params=pltpu.CompilerParams(
            dimension_semantics=("parallel", "parallel"), vmem_limit_bytes=VMEM_LIMIT_MID),
        name="nsa_layout",
    )(p, cos, sin)


def rope_tables(pos):
    half = HEAD_DIM // 2
    inv = ROPE_THETA ** (-jnp.arange(half, dtype=F32) / half)
    ang = pos.astype(F32)[:, None] * inv[None, :]
    cos = jnp.cos(ang)
    sin = jnp.sin(ang)
    return jnp.concatenate([cos, cos], axis=1), jnp.concatenate([-sin, sin], axis=1)


CMP_IN = CMP_BLOCK * HEAD_DIM
CMP_HIDDEN = 256


def _compress_rows(x, pe_ref, w1_ref, w2_ref):
    kb = (x + pe_ref[...]).astype(BF16)
    hid = _gelu_tanh(jnp.dot(kb, w1_ref[...], preferred_element_type=F32))
    return jnp.dot(hid.astype(BF16), w2_ref[...], preferred_element_type=F32)


def _compress_kernel(blk_ref, x_ref, pe_ref, w1_ref, w2_ref, o_ref):
    del blk_ref
    o_ref[...] = _compress_rows(x_ref[...], pe_ref, w1_ref, w2_ref)


def compress_blocks(x, pe, w1, w2, *, tr, blocks):
    blocks = jnp.asarray(blocks, jnp.int32)
    nb = blocks.shape[0]
    return pl.pallas_call(
        _compress_kernel,
        grid_spec=pltpu.PrefetchScalarGridSpec(
            num_scalar_prefetch=1,
            grid=(nb,),
            in_specs=[
                pl.BlockSpec((tr, CMP_IN), lambda i, blk: (blk[i], 0)),
                pl.BlockSpec((1, CMP_IN), lambda i, blk: (0, 0)),
                pl.BlockSpec((CMP_IN, CMP_HIDDEN), lambda i, blk: (0, 0)),
                pl.BlockSpec((CMP_HIDDEN, HEAD_DIM), lambda i, blk: (0, 0)),
            ],
            out_specs=pl.BlockSpec((tr, HEAD_DIM), lambda i, blk: (i, 0)),
        ),
        out_shape=jax.ShapeDtypeStruct((nb * tr, HEAD_DIM), F32),
        compiler_params=pltpu.CompilerParams(
            dimension_semantics=("arbitrary",), vmem_limit_bytes=VMEM_LIMIT_MID),
        name="compress_blocks",
    )(blocks, x, pe.reshape(1, CMP_IN), w1, w2)


ATT_TQ = 256
ATT_TK = 256


def _softmax_masked(s, mask, axis):
    sm = jnp.where(mask, s, NEG_INF)
    e = jnp.exp(sm - jnp.max(sm, axis=axis, keepdims=True))
    p = e / jnp.sum(e, axis=axis, keepdims=True)
    return jnp.where(mask, p, 0.0)


def _select_blocks(score_t):
    nb = score_t.shape[0]
    blk = lax.broadcasted_iota(jnp.int32, score_t.shape, 0)
    rank = jnp.zeros(score_t.shape, F32)
    for m in range(nb):
        row = score_t[m:m + 1, :]
        earlier = jnp.where(blk > m, 1.0, 0.0)
        rank = rank + jnp.where(row > score_t, 1.0, jnp.where(row == score_t, earlier, 0.0))
    return jnp.where(rank < float(N_SELECT), 1.0, 0.0)


def _flash(q2, k_tile, v_tile, kt_lo, kt_hi, allowed, heads, tq):
    r = heads * tq

    def body(kt, carry):
        m, l, acc = carry
        s = lax.dot_general(q2, k_tile(kt), NT_DIMS, preferred_element_type=F32).reshape(heads, tq, ATT_TK)
        ok = allowed(kt)[None]
        s = jnp.where(ok, s, NEG_INF)
        m_new = jnp.maximum(m, jnp.max(s, axis=-1, keepdims=True))
        alpha = jnp.exp(m - m_new)
        p = jnp.where(ok, jnp.exp(s - m_new), 0.0)
        l = alpha * l + jnp.sum(p, axis=-1, keepdims=True)
        pv = jnp.dot(p.reshape(r, ATT_TK).astype(BF16), v_tile(kt), preferred_element_type=F32)
        acc = alpha * acc + pv.reshape(heads, tq, HEAD_DIM)
        return m_new, l, acc

    init = (jnp.full((heads, tq, 1), NEG_INF, F32), jnp.zeros((heads, tq, 1), F32),
            jnp.zeros((heads, tq, HEAD_DIM), F32))
    _, l, acc = lax.fori_loop(kt_lo, kt_hi, body, init)
    return jnp.where(l > 0.0, acc / jnp.where(l > 0.0, l, 1.0), 0.0)


def _nsa_prompt_attn_kernel(q_ref, kc_ref, vc_ref, ks_ref, vs_ref, kw_ref, vw_ref, gate_ref, o_ref):
    tq = ATT_TQ
    qi = pl.program_id(2)
    q0 = qi * tq
    q3 = q_ref[0]
    q2 = q3.reshape(HPG * tq, HEAD_DIM)
    nb = kc_ref.shape[2]
    kc = kc_ref[0, 0].astype(BF16)
    vc = vc_ref[0, 0].astype(BF16)

    pos_col = q0 + lax.broadcasted_iota(jnp.int32, (tq, 1), 0)
    blk_row = lax.broadcasted_iota(jnp.int32, (1, nb), 1)
    c_mask = ((blk_row + 1) * CMP_BLOCK - 1) <= pos_col
    s_c = lax.dot_general(q2, kc, NT_DIMS, preferred_element_type=F32).reshape(HPG, tq, nb)
    p_c = _softmax_masked(s_c, c_mask[None], axis=-1)
    o_c = jnp.dot(p_c.reshape(HPG * tq, nb).astype(BF16), vc, preferred_element_type=F32).reshape(HPG, tq, HEAD_DIM)

    pos_row = q0 + lax.broadcasted_iota(jnp.int32, (1, tq), 1)
    blk_col = lax.broadcasted_iota(jnp.int32, (nb, 1), 0)
    c_mask_t = ((blk_col + 1) * CMP_BLOCK - 1) <= pos_row
    imp = jnp.zeros((nb, tq), F32)
    for h in range(HPG):
        s_t = lax.dot_general(kc, q3[h], NT_DIMS, preferred_element_type=F32)
        imp = imp + _softmax_masked(s_t, c_mask_t, axis=0)
    cur = pos_row // CMP_BLOCK
    reach = blk_col <= cur
    forced = jnp.logical_and(reach, jnp.logical_or(blk_col == 0, blk_col >= cur - 1))
    score = jnp.where(forced, FORCED_SCORE, jnp.where(reach, imp, NEG_INF))
    sel_t = _select_blocks(score).astype(BF16)
    eye = jnp.where(lax.broadcasted_iota(jnp.int32, (tq, tq), 0) == lax.broadcasted_iota(jnp.int32, (tq, tq), 1),
                    1.0, 0.0).astype(BF16)
    sel = lax.dot_general(eye, sel_t, NT_DIMS, preferred_element_type=F32).astype(BF16)

    def key_pos(kt):
        return kt * ATT_TK + lax.broadcasted_iota(jnp.int32, (1, ATT_TK), 1)

    def tile_of(ref, lead):
        def get(kt):
            return ref[lead + (pl.ds(pl.multiple_of(kt * ATT_TK, ATT_TK), ATT_TK), slice(None))].astype(BF16)
        return get

    def sel_allowed(kt):
        kpos = key_pos(kt)
        expand = jnp.where((kpos // CMP_BLOCK) == blk_col, 1.0, 0.0).astype(BF16)
        in_sel = jnp.dot(sel, expand, preferred_element_type=F32) > 0.5
        return jnp.logical_and(in_sel, kpos <= pos_col)

    def win_allowed(kt):
        dist = pos_col - key_pos(kt)
        return jnp.logical_and(dist >= 0, dist <= WINDOW)

    lead_s = (0, 0, 0, 0)
    o_s = _flash(q2, tile_of(ks_ref, lead_s), tile_of(vs_ref, lead_s), 0, qi + 1, sel_allowed, HPG, tq)
    lead_w = (0, 0, 0)
    kt_lo = jnp.maximum(qi - WINDOW // ATT_TK, 0)
    o_w = _flash(q2, tile_of(kw_ref, lead_w), tile_of(vw_ref, lead_w), kt_lo, qi + 1, win_allowed, HPG, tq)

    gate = jax.nn.sigmoid(gate_ref[0, 0])
    for h in range(HPG):
        o = (gate[:, 3 * h:3 * h + 1] * o_c[h] + gate[:, 3 * h + 1:3 * h + 2] * o_s[h]
             + gate[:, 3 * h + 2:3 * h + 3] * o_w[h])
        o_ref[:, h * HEAD_DIM:(h + 1) * HEAD_DIM] = o


def nsa_prompt_attn(q, kc, vc, rows, win, gates):
    b, _, t, _ = q.shape
    nb = kc.shape[2]
    n_q = t // ATT_TQ
    kv_spec = lambda comp: pl.BlockSpec((1, 1, 1, 1, t, HEAD_DIM), lambda bi, g, qi: (bi, 0, comp, g, 0, 0))
    win_spec = lambda comp: pl.BlockSpec((1, 1, 1, t, HEAD_DIM), lambda bi, g, qi: (bi, comp, g, 0, 0))
    cmp_spec = pl.BlockSpec((1, 1, nb, HEAD_DIM), lambda bi, g, qi: (bi, g, 0, 0))
    return pl.pallas_call(
        _nsa_prompt_attn_kernel,
        grid=(b, N_KV_HEADS, n_q),
        in_specs=[
            pl.BlockSpec((1, HPG, ATT_TQ, HEAD_DIM), lambda bi, g, qi: (bi, g, qi, 0)),
            cmp_spec, cmp_spec,
            kv_spec(2), kv_spec(3),
            win_spec(0), win_spec(1),
            pl.BlockSpec((1, 1, ATT_TQ, 3 * HPG), lambda bi, g, qi: (bi, g, qi, 0)),
        ],
        out_specs=pl.BlockSpec((ATT_TQ, HPG * HEAD_DIM), lambda bi, g, qi: (bi * n_q + qi, g)),
        out_shape=jax.ShapeDtypeStruct((b * t, Q_DIM), F32),
        compiler_params=pltpu.CompilerParams(
            dimension_semantics=("parallel", "parallel", "arbitrary"), vmem_limit_bytes=VMEM_LIMIT_MID),
        name="nsa_prompt_attn",
    )(q, kc, vc, rows, rows, win, win, gates)


PAGE_SIZE = 128
DEC_ROWS = 512
DEC_NB_PAD = 128
NSA_PROJ_PAD = 5632
BLOCKS_PER_PAGE = PAGE_SIZE // CMP_BLOCK
PAGE_ROWS = N_KV_HEADS * BLOCKS_PER_PAGE
CMP_PAGES_PER_STEP = 32


def _paged_compress_kernel(pt_ref, cache_ref, pe_ref, w1_ref, w2_ref, o_ref, buf, sem, *, comp):
    s = pl.program_id(0)

    def page_copy(p):
        page = pt_ref[s * CMP_PAGES_PER_STEP + p]
        return pltpu.make_async_copy(cache_ref.at[page, comp], buf.at[p], sem.at[0])

    for p in range(CMP_PAGES_PER_STEP):
        page_copy(p).start()
    for p in range(CMP_PAGES_PER_STEP):
        page_copy(p).wait()
    x = buf[...].reshape(CMP_PAGES_PER_STEP * PAGE_ROWS, CMP_IN)
    o_ref[...] = _compress_rows(x, pe_ref, w1_ref, w2_ref)


def paged_compress(page_table_flat, cache, pe, w1, w2, *, comp):
    n_pages = page_table_flat.shape[0]
    assert n_pages % CMP_PAGES_PER_STEP == 0
    rows = CMP_PAGES_PER_STEP * PAGE_ROWS
    return pl.pallas_call(
        functools.partial(_paged_compress_kernel, comp=comp),
        grid_spec=pltpu.PrefetchScalarGridSpec(
            num_scalar_prefetch=1,
            grid=(n_pages // CMP_PAGES_PER_STEP,),
            in_specs=[
                pl.BlockSpec(memory_space=pl.ANY),
                pl.BlockSpec((1, CMP_IN), lambda i, pt: (0, 0)),
                pl.BlockSpec((CMP_IN, CMP_HIDDEN), lambda i, pt: (0, 0)),
                pl.BlockSpec((CMP_HIDDEN, HEAD_DIM), lambda i, pt: (0, 0)),
            ],
            out_specs=pl.BlockSpec((rows, HEAD_DIM), lambda i, pt: (i, 0)),
            scratch_shapes=[
                pltpu.VMEM((CMP_PAGES_PER_STEP, PAGE_ROWS, CMP_IN), F32),
                pltpu.SemaphoreType.DMA((1,)),
            ],
        ),
        out_shape=jax.ShapeDtypeStruct((n_pages * PAGE_ROWS, HEAD_DIM), F32),
        compiler_params=pltpu.CompilerParams(
            dimension_semantics=("arbitrary",), vmem_limit_bytes=VMEM_LIMIT_MID),
        name="paged_compress",
    )(page_table_flat, cache, pe.reshape(1, CMP_IN), w1, w2)


def _group_diag(rows_per_group, cols_per_group, shape):
    r = lax.broadcasted_iota(jnp.int32, shape, 0) // rows_per_group
    c = lax.broadcasted_iota(jnp.int32, shape, 1) // cols_per_group
    return r == c


def _decode_attn_a_kernel(q_ref, kc_ref, vc_ref, new_ref, gate_ref, win_ref,
                          ocw_ref, sel_ref, wout_ref, *, nb, nb_pad, pos):
    q = q_ref[0]
    kc = kc_ref[0].reshape(N_KV_HEADS * nb_pad, HEAD_DIM).astype(BF16)
    vc = vc_ref[0].reshape(N_KV_HEADS * nb_pad, HEAD_DIM).astype(BF16)
    gate = jax.nn.sigmoid(gate_ref[0])

    shape = (N_HEADS, N_KV_HEADS * nb_pad)
    blk = lax.broadcasted_iota(jnp.int32, shape, 1) % nb_pad
    complete = jnp.logical_and((blk + 1) * CMP_BLOCK - 1 <= pos, blk < nb)
    c_mask = jnp.logical_and(_group_diag(HPG, nb_pad, shape), complete)
    s_c = lax.dot_general(q, kc, NT_DIMS, preferred_element_type=F32)
    p_c = _softmax_masked(s_c, c_mask, axis=-1)
    o_c = jnp.dot(p_c.astype(BF16), vc, preferred_element_type=F32)

    blk_col = lax.broadcasted_iota(jnp.int32, (nb_pad, 1), 0)
    complete_t = jnp.logical_and((blk_col + 1) * CMP_BLOCK - 1 <= pos, blk_col < nb)
    lane16 = lax.broadcasted_iota(jnp.int32, (nb_pad, N_HEADS), 1)
    lane = lax.broadcasted_iota(jnp.int32, (nb_pad, 128), 1)
    imp = jnp.zeros((nb_pad, 128), F32)
    for g in range(N_KV_HEADS):
        s_t = lax.dot_general(kc[g * nb_pad:(g + 1) * nb_pad], q, NT_DIMS, preferred_element_type=F32)
        p_t = _softmax_masked(s_t, complete_t, axis=0)
        imp_g = jnp.sum(jnp.where(lane16 // HPG == g, p_t, 0.0), axis=1, keepdims=True)
        imp = jnp.where(lane == g, imp_g, imp)
    cur = pos // CMP_BLOCK
    reach = jnp.logical_and(blk_col <= cur, blk_col < nb)
    forced = jnp.logical_and(reach, jnp.logical_or(blk_col == 0, blk_col >= cur - 1))
    score = jnp.where(forced, FORCED_SCORE, jnp.where(reach, imp, NEG_INF))
    score = jnp.where(blk_col < nb, score, 2.0 * NEG_INF)
    sel_t = _select_blocks(score).astype(BF16)
    head_group = jnp.where(
        lax.broadcasted_iota(jnp.int32, (N_HEADS, 128), 0) // HPG == lax.broadcasted_iota(jnp.int32, (N_HEADS, 128), 1),
        1.0, 0.0).astype(BF16)
    sel_ref[0] = lax.dot_general(head_group, sel_t, NT_DIMS, preferred_element_type=F32)

    wb = win_ref.shape[4]
    kw = win_ref[0, 0, 0].reshape(N_KV_HEADS * wb, HEAD_DIM).astype(BF16)
    vw = win_ref[0, 0, 1].reshape(N_KV_HEADS * wb, HEAD_DIM).astype(BF16)
    k_new = new_ref[0, 4].astype(BF16).astype(F32)
    v_new = new_ref[0, 5].astype(BF16).astype(F32)
    shape_w = (N_HEADS, N_KV_HEADS * wb)
    w_mask = _group_diag(HPG, wb, shape_w)
    s_w = jnp.where(w_mask, lax.dot_general(q, kw, NT_DIMS, preferred_element_type=F32), NEG_INF)
    s_n = jnp.sum(q.astype(F32) * k_new, axis=-1, keepdims=True)
    m = jnp.maximum(jnp.max(s_w, axis=-1, keepdims=True), s_n)
    e_w = jnp.where(w_mask, jnp.exp(s_w - m), 0.0)
    e_n = jnp.exp(s_n - m)
    denom = jnp.sum(e_w, axis=-1, keepdims=True) + e_n
    o_w = (jnp.dot(e_w.astype(BF16), vw, preferred_element_type=F32)
           + e_n.astype(BF16).astype(F32) * v_new) / denom

    ocw_ref[0] = gate[:, 0:1] * o_c + gate[:, 2:3] * o_w

    for c in range(2):
        for g in range(N_KV_HEADS):
            wout_ref[0, 0, c, g, 0:wb - 1, :] = win_ref[0, 0, c, g, 1:wb, :]
            wout_ref[0, 0, c, g, wb - 1:wb, :] = new_ref[0, 4 + c, g * HPG:g * HPG + 1, :]


def decode_attn_a(q, kc, vc, new_rows, gates, win, *, nb, pos):
    s = q.shape[0]
    nb_pad = kc.shape[2]
    wb = win.shape[4]
    cmp_spec = pl.BlockSpec((1, N_KV_HEADS, nb_pad, HEAD_DIM), lambda i: (i, 0, 0, 0))
    win_spec = pl.BlockSpec((1, 1, 2, N_KV_HEADS, wb, HEAD_DIM), lambda i: (i, 0, 0, 0, 0, 0))
    return pl.pallas_call(
        functools.partial(_decode_attn_a_kernel, nb=nb, nb_pad=nb_pad, pos=pos),
        grid=(s,),
        in_specs=[
            pl.BlockSpec((1, N_HEADS, HEAD_DIM), lambda i: (i, 0, 0)),
            cmp_spec, cmp_spec,
            pl.BlockSpec((1, 6, N_HEADS, HEAD_DIM), lambda i: (i, 0, 0, 0)),
            pl.BlockSpec((1, N_HEADS, 3), lambda i: (i, 0, 0)),
            win_spec,
        ],
        out_specs=[
            pl.BlockSpec((1, N_HEADS, HEAD_DIM), lambda i: (i, 0, 0)),
            pl.BlockSpec((1, N_HEADS, nb_pad), lambda i: (i, 0, 0)),
            win_spec,
        ],
        out_shape=[
            jax.ShapeDtypeStruct((s, N_HEADS, HEAD_DIM), F32),
            jax.ShapeDtypeStruct((s, N_HEADS, nb_pad), F32),
            jax.ShapeDtypeStruct(win.shape, F32),
        ],
        compiler_params=pltpu.CompilerParams(
            dimension_semantics=("parallel",), vmem_limit_bytes=VMEM_LIMIT_MID),
        name="decode_attn_a",
    )(q, kc, vc, new_rows, gates, win)


def _decode_attn_b_kernel(pt_ref, q_ref, sel_ref, kv_ref, new_ref, gate_ref, ocw_ref, o_ref,
                          m_scr, l_scr, acc_scr, *, pos):
    del pt_ref
    pg = pl.program_id(1)
    n_pg = pl.num_programs(1)
    q = q_ref[0]
    sel = sel_ref[0].astype(BF16)
    nb_pad = sel.shape[1]

    @pl.when(pg == 0)
    def _():
        m_scr[...] = jnp.full(m_scr.shape, NEG_INF, F32)
        l_scr[...] = jnp.zeros(l_scr.shape, F32)
        acc_scr[...] = jnp.zeros(acc_scr.shape, F32)

    def update(s, ok, weigh):
        s = jnp.where(ok, s, NEG_INF)
        m_old = m_scr[...]
        m_new = jnp.maximum(m_old, jnp.max(s, axis=-1, keepdims=True))
        alpha = jnp.exp(m_old - m_new)
        p = jnp.where(ok, jnp.exp(s - m_new), 0.0)
        l_scr[...] = alpha * l_scr[...] + jnp.sum(p, axis=-1, keepdims=True)
        acc_scr[...] = alpha * acc_scr[...] + weigh(p.astype(BF16))
        m_scr[...] = m_new

    n_keys = N_KV_HEADS * PAGE_SIZE
    k = kv_ref[0, 0, 0].reshape(n_keys, HEAD_DIM).astype(BF16)
    v = kv_ref[0, 0, 1].reshape(n_keys, HEAD_DIM).astype(BF16)
    shape = (N_HEADS, n_keys)
    key_in_page = lax.broadcasted_iota(jnp.int32, (nb_pad, n_keys), 1) % PAGE_SIZE
    key_blk = pg * BLOCKS_PER_PAGE + key_in_page // CMP_BLOCK
    expand = jnp.where(key_blk == lax.broadcasted_iota(jnp.int32, (nb_pad, n_keys), 0), 1.0, 0.0).astype(BF16)
    in_sel = jnp.dot(sel, expand, preferred_element_type=F32) > 0.5
    key_pos = pg * PAGE_SIZE + lax.broadcasted_iota(jnp.int32, shape, 1) % PAGE_SIZE
    ok = jnp.logical_and(jnp.logical_and(in_sel, key_pos <= pos), _group_diag(HPG, PAGE_SIZE, shape))
    update(lax.dot_general(q, k, NT_DIMS, preferred_element_type=F32), ok,
           lambda p: jnp.dot(p, v, preferred_element_type=F32))

    @pl.when(pg == n_pg - 1)
    def _():
        new_blk = pos // CMP_BLOCK
        k_new = new_ref[0, 2].astype(BF16).astype(F32)
        v_new = new_ref[0, 3].astype(BF16).astype(F32)
        sel_new = jnp.sum(jnp.where(lax.broadcasted_iota(jnp.int32, sel.shape, 1) == new_blk,
                                    sel_ref[0], 0.0), axis=-1, keepdims=True) > 0.5
        update(jnp.sum(q.astype(F32) * k_new, axis=-1, keepdims=True), sel_new,
               lambda p: p.astype(F32) * v_new)
        l = l_scr[...]
        o_s = jnp.where(l > 0.0, acc_scr[...] / jnp.where(l > 0.0, l, 1.0), 0.0)
        gate = jax.nn.sigmoid(gate_ref[0])
        o_ref[0] = ocw_ref[0] + gate[:, 1:2] * o_s


def decode_attn_b(page_table_flat, q, sel, cache, new_rows, gates, ocw, *, pages_per_seq, pos):
    s = q.shape[0]
    nb_pad = sel.shape[2]
    seq_spec = lambda shape: pl.BlockSpec((1,) + shape, lambda i, pg, pt: (i,) + (0,) * len(shape))
    return pl.pallas_call(
        functools.partial(_decode_attn_b_kernel, pos=pos),
        grid_spec=pltpu.PrefetchScalarGridSpec(
            num_scalar_prefetch=1,
            grid=(s, pages_per_seq),
            in_specs=[
                seq_spec((N_HEADS, HEAD_DIM)),
                seq_spec((N_HEADS, nb_pad)),
                pl.BlockSpec((1, 1, 2, N_KV_HEADS, PAGE_SIZE, HEAD_DIM),
                             lambda i, pg, pt: (pt[i * pages_per_seq + pg], 0, 1, 0, 0, 0)),
                seq_spec((6, N_HEADS, HEAD_DIM)),
                seq_spec((N_HEADS, 3)),
                seq_spec((N_HEADS, HEAD_DIM)),
            ],
            out_specs=seq_spec((N_HEADS, HEAD_DIM)),
            scratch_shapes=[
                pltpu.VMEM((N_HEADS, 1), F32),
                pltpu.VMEM((N_HEADS, 1), F32),
                pltpu.VMEM((N_HEADS, HEAD_DIM), F32),
            ],
        ),
        out_shape=jax.ShapeDtypeStruct((s, N_HEADS, HEAD_DIM), F32),
        compiler_params=pltpu.CompilerParams(
            dimension_semantics=("parallel", "arbitrary"), vmem_limit_bytes=VMEM_LIMIT_MID),
        name="decode_attn_b",
    )(page_table_flat, q, sel, cache, new_rows, gates, ocw)


def _matmul_residual_kernel(a_ref, a2_ref, w_ref, x_ref, o_ref, *, first_tiles):
    i = pl.program_id(0)

    @pl.when(i < first_tiles)
    def _():
        o_ref[...] = x_ref[...] + jnp.dot(a_ref[...].astype(BF16), w_ref[...], preferred_element_type=F32)

    @pl.when(i >= first_tiles)
    def _():
        o_ref[...] = x_ref[...] + jnp.dot(a2_ref[...].astype(BF16), w_ref[...], preferred_element_type=F32)


def matmul_residual(a, a2, w, x, *, tm):
    n1, k = a.shape
    n2 = a2.shape[0]
    n, d = x.shape
    assert n == n1 + n2 and n1 % tm == 0 and n2 % tm == 0
    first = n1 // tm
    return pl.pallas_call(
        functools.partial(_matmul_residual_kernel, first_tiles=first),
        grid=(n // tm,),
        in_specs=[
            pl.BlockSpec((tm, k), lambda i: (jnp.minimum(i, first - 1), 0)),
            pl.BlockSpec((tm, k), lambda i: (jnp.maximum(i - first, 0), 0)),
            pl.BlockSpec((k, d), lambda i: (0, 0)),
            pl.BlockSpec((tm, d), lambda i: (i, 0)),
        ],
        out_specs=pl.BlockSpec((tm, d), lambda i: (i, 0)),
        out_shape=jax.ShapeDtypeStruct((n, d), F32),
        compiler_params=pltpu.CompilerParams(
            dimension_semantics=("parallel",), vmem_limit_bytes=VMEM_LIMIT_MID),
        name="matmul_residual",
    )(a, a2, w, x)


def _rmsnorm_kernel(x_ref, g_ref, o_ref):
    x = x_ref[...]
    ms = jnp.mean(x * x, axis=-1, keepdims=True)
    o_ref[...] = x * lax.rsqrt(ms + RMS_EPS) * g_ref[...]


def rmsnorm_rows(x, g, *, row0, n_rows, tm):
    d = x.shape[1]
    assert row0 % tm == 0 and n_rows % tm == 0
    blk0 = row0 // tm
    return pl.pallas_call(
        _rmsnorm_kernel,
        grid=(n_rows // tm,),
        in_specs=[
            pl.BlockSpec((tm, d), lambda i: (blk0 + i, 0)),
            pl.BlockSpec((1, d), lambda i: (0, 0)),
        ],
        out_specs=pl.BlockSpec((tm, d), lambda i: (i, 0)),
        out_shape=jax.ShapeDtypeStruct((n_rows, d), F32),
        compiler_params=pltpu.CompilerParams(dimension_semantics=("parallel",)),
        name="rmsnorm_rows",
    )(x, g.reshape(1, d))


def kernel(x_prompt, x_sample, state_conv, cache_nsa_kv, state_nsa_win, page_table, norm_mix, norm_ffn, norm_final, conv_w_in, conv_w, conv_w_out, nsa_w_in, nsa_w_out, nsa_cmp_pe_k, nsa_cmp_w1_k, nsa_cmp_w2_k, nsa_cmp_pe_v, nsa_cmp_w1_v, nsa_cmp_w2_v, peer_w_q, peer_sub_keys, peer_u, peer_v):
    b, t, d = x_prompt.shape
    s = x_sample.shape[0]
    assert x_sample.shape[1] == 1 and d == D_MODEL
    n_prompt = b * t
    n_dec = DEC_ROWS
    pages_per_seq = page_table.shape[1]
    past = pages_per_seq * PAGE_SIZE
    nb_dec = past // CMP_BLOCK + 1

    def peer(x, i):
        return peer_layer(
            x, norm_ffn[i], peer_w_q[i].astype(BF16),
            peer_sub_keys[i].reshape(2 * PEER_HEADS, PEER_KEYS, PEER_HALF_DIM).astype(BF16),
            peer_u[i].astype(BF16), peer_v[i].astype(BF16).T)

    x = jnp.concatenate([x_prompt.reshape(n_prompt, d), x_sample.reshape(s, d),
                         jnp.zeros((n_dec - s, d), F32)], axis=0)

    p = rms_matmul(x, norm_mix[0], conv_w_in[0].astype(BF16), tm=512, tn=1024)
    st = state_conv[:, 0]
    pad_dec = lambda a: jnp.pad(a, ((0, n_dec - s), (0, 0)))
    x, tail, u_dec = conv_core(p, x, pad_dec(st[:, 0]), pad_dec(st[:, 1]), conv_w[0],
                               conv_w_out[0].astype(BF16), n_prompt=n_prompt, seq_len=t)
    tail_p = tail[:n_prompt // CONV_TM].reshape(b, t // CONV_TM, CONV_TAIL, d)
    conv_state_prompt = tail_p[:, -1, CONV_TAIL - 2:][:, None]
    conv_state_sample = jnp.stack([st[:, 1], u_dec[:s]], axis=1)[:, None]
    x = peer(x, 0)

    proj = Q_DIM + 6 * KV_DIM
    n_gate = 3 * N_HEADS
    w_in = jnp.pad(nsa_w_in[0], ((0, 0), (0, NSA_PROJ_PAD - proj - n_gate))).astype(BF16)
    pn = rms_matmul(x, norm_mix[1], w_in, tm=512, tn=512)
    gates = pn[:, proj:proj + n_gate]
    w1k, w2k = nsa_cmp_w1_k[0].astype(BF16), nsa_cmp_w2_k[0].astype(BF16)
    w1v, w2v = nsa_cmp_w1_v[0].astype(BF16), nsa_cmp_w2_v[0].astype(BF16)
    pe_k, pe_v = nsa_cmp_pe_k[0], nsa_cmp_pe_v[0]

    cos_p, sin_p = rope_tables(jnp.arange(t))
    q_p, rows_p, win_p = nsa_layout(pn, cos_p, sin_p, n_seq=b, seq_len=t, row0=0, tq=ATT_TQ)
    nb = t // CMP_BLOCK
    flat = rows_p.reshape(b * 4 * N_KV_HEADS * nb, CMP_IN)
    tr = N_KV_HEADS * nb
    kc_p = compress_blocks(flat, pe_k, w1k, w2k, tr=tr, blocks=[4 * i for i in range(b)])
    vc_p = compress_blocks(flat, pe_v, w1v, w2v, tr=tr, blocks=[4 * i + 1 for i in range(b)])
    gates_p = gates[:n_prompt].reshape(b, t, N_KV_HEADS, 3 * HPG).transpose(0, 2, 1, 3)
    o_p = nsa_prompt_attn(q_p, kc_p.reshape(b, N_KV_HEADS, nb, HEAD_DIM), vc_p.reshape(b, N_KV_HEADS, nb, HEAD_DIM),
                          rows_p, win_p, gates_p)
    wb_p = min(WINDOW, t)
    nsa_win_prompt = win_p[:, None, :, :, t - wb_p:, :]

    cos_s, sin_s = rope_tables(jnp.full((s,), past, jnp.int32))
    q_s, rows_s, win_s = nsa_layout(pn, cos_s, sin_s, n_seq=1, seq_len=s, row0=n_prompt, tq=s)
    q_s = q_s[0].transpose(1, 0, 2)
    new_rows = jnp.concatenate([rows_s[0, 0], win_s[0]], axis=0).transpose(2, 0, 1, 3)
    nsa_rows_sample = new_rows[:, None, :4, :, None, :]
    new16 = jnp.repeat(new_rows, HPG, axis=2)
    pt_flat = page_table.reshape(-1)
    cache4 = cache_nsa_kv.reshape(cache_nsa_kv.shape[0], 4, PAGE_ROWS, CMP_IN)

    def decode_summaries(comp, pe, w1, w2):
        past_c = paged_compress(pt_flat, cache4, pe, w1, w2, comp=comp)
        past_c = past_c.reshape(s, pages_per_seq, N_KV_HEADS, BLOCKS_PER_PAGE, HEAD_DIM)
        past_c = past_c.transpose(0, 2, 1, 3, 4).reshape(s, N_KV_HEADS, pages_per_seq * BLOCKS_PER_PAGE, HEAD_DIM)
        new_blk = jnp.pad(new_rows[:, comp].reshape(s * N_KV_HEADS, HEAD_DIM), ((0, 0), (0, CMP_IN - HEAD_DIM)))
        new_c = compress_blocks(new_blk, pe, w1, w2, tr=s * N_KV_HEADS, blocks=[0])
        both = jnp.concatenate([past_c, new_c.reshape(s, N_KV_HEADS, 1, HEAD_DIM)], axis=2)
        return jnp.pad(both, ((0, 0), (0, 0), (0, DEC_NB_PAD - nb_dec), (0, 0)))

    kc_s = decode_summaries(0, pe_k, w1k, w2k)
    vc_s = decode_summaries(1, pe_v, w1v, w2v)
    gates_s = gates[n_prompt:n_prompt + s].reshape(s, N_HEADS, 3)
    ocw, sel, nsa_win_sample = decode_attn_a(q_s, kc_s, vc_s, new16, gates_s, state_nsa_win, nb=nb_dec, pos=past)
    o_s = decode_attn_b(pt_flat, q_s, sel, cache_nsa_kv, new16, gates_s, ocw, pages_per_seq=pages_per_seq, pos=past)
    o_s = jnp.pad(o_s.reshape(s, Q_DIM), ((0, n_dec - s), (0, 0)))

    x = matmul_residual(o_p, o_s, nsa_w_out[0].astype(BF16), x, tm=256)
    x = peer(x, 1)

    y_prompt = rmsnorm_rows(x, norm_final, row0=0, n_rows=n_prompt, tm=512).reshape(b, t, d)
    y_sample = rmsnorm_rows(x, norm_final, row0=n_prompt, n_rows=s, tm=s).reshape(s, 1, d)
    return (y_prompt, y_sample, conv_state_prompt, conv_state_sample,
            rows_p, nsa_rows_sample, nsa_win_prompt, nsa_win_sample)
```

```python
import functools
import math

import jax
import jax.numpy as jnp
from jax import lax
from jax.experimental import pallas as pl
from jax.experimental.pallas import tpu as pltpu

F32 = jnp.float32
BF16 = jnp.bfloat16

D_MODEL = 2048
RMS_EPS = 1e-6

PEER_HEADS = 8
PEER_KEYS = 128
PEER_TOPK = 16
PEER_HALF_DIM = 128
PEER_EXPERTS = PEER_KEYS * PEER_KEYS

VMEM_LIMIT_BIG = 60 * 1024 * 1024
VMEM_LIMIT_MID = 48 * 1024 * 1024

NT_DIMS = (((1,), (1,)), ((), ()))


def _gelu_tanh(x):
    c = math.sqrt(2.0 / math.pi)
    return 0.5 * x * (1.0 + jnp.tanh(c * (x + 0.044715 * (x * x * x))))


def _rms_matmul_kernel(x_ref, g_ref, w_ref, o_ref, *rest, emit_ht):
    if emit_ht:
        ht_ref, h_scr = rest
    else:
        (h_scr,) = rest

    @pl.when(pl.program_id(1) == 0)
    def _():
        x = x_ref[...]
        ms = jnp.mean(x * x, axis=-1, keepdims=True)
        h = x * lax.rsqrt(ms + RMS_EPS) * g_ref[...]
        h_scr[...] = h.astype(BF16)
        if emit_ht:
            ht_ref[...] = h.T.astype(BF16)

    o_ref[...] = jnp.dot(h_scr[...], w_ref[...], preferred_element_type=F32)


def rms_matmul(x, g, w, *, tm, tn, emit_ht=False):
    n, d = x.shape
    m = w.shape[1]
    assert n % tm == 0 and m % tn == 0
    out_shape = [jax.ShapeDtypeStruct((n, m), F32)]
    out_specs = [pl.BlockSpec((tm, tn), lambda i, j: (i, j))]
    if emit_ht:
        out_shape.append(jax.ShapeDtypeStruct((d, n), BF16))
        out_specs.append(pl.BlockSpec((d, tm), lambda i, j: (0, i)))
    res = pl.pallas_call(
        functools.partial(_rms_matmul_kernel, emit_ht=emit_ht),
        grid=(n // tm, m // tn),
        in_specs=[
            pl.BlockSpec((tm, d), lambda i, j: (i, 0)),
            pl.BlockSpec((1, d), lambda i, j: (0, 0)),
            pl.BlockSpec((d, tn), lambda i, j: (0, j)),
        ],
        out_specs=out_specs,
        out_shape=out_shape,
        scratch_shapes=[pltpu.VMEM((tm, d), BF16)],
        compiler_params=pltpu.CompilerParams(
            dimension_semantics=("parallel", "arbitrary"), vmem_limit_bytes=VMEM_LIMIT_MID),
        name="rms_matmul_ht" if emit_ht else "rms_matmul",
    )(x, g.reshape(1, d), w)
    return res if emit_ht else res[0]


NEG_BIG = -jnp.inf
NOT_RANKED = 99.0


def _top16_rows(s):
    rows = lax.broadcasted_iota(jnp.int32, s.shape, 0).astype(F32)
    rank = jnp.full(s.shape, NOT_RANKED, F32)
    vals = []
    for k in range(PEER_TOPK):
        m = jnp.max(s, axis=0, keepdims=True)
        idx = jnp.min(jnp.where(s == m, rows, float(PEER_KEYS)), axis=0, keepdims=True)
        hit = rows == idx
        rank = jnp.where(hit, float(k), rank)
        s = jnp.where(hit, NEG_BIG, s)
        vals.append(m)
    return vals, rank


def _pair_counts(a, b):
    t = a[0].shape[1]
    b16 = jnp.concatenate(b, axis=0)
    a_hi = jnp.concatenate(a[8:], axis=0)
    sub8 = lax.broadcasted_iota(jnp.int32, (8, t), 0).astype(F32)
    sub16 = lax.broadcasted_iota(jnp.int32, (16, t), 0).astype(F32)
    cands = [a[0] + b16]
    flats = [sub16]
    for k0 in range(1, 8):
        lim = PEER_TOPK // (k0 + 1)
        c = a[k0] + b16[:8]
        if lim < 8:
            c = jnp.where(sub8 < float(lim), c, NEG_BIG)
        cands.append(c)
        flats.append(sub8 + float(k0 * 16))
    cands.append(a_hi + b[0])
    flats.append((sub8 + 8.0) * 16.0)

    cmax = a[0] + b[0]
    counts = [jnp.zeros(c.shape, F32) for c in cands]
    z = jnp.zeros((1, t), F32)
    for _ in range(PEER_TOPK):
        m = cands[0].max(axis=0, keepdims=True)
        for c in cands[1:]:
            m = jnp.maximum(m, c.max(axis=0, keepdims=True))
        idx = None
        for c, f in zip(cands, flats):
            i = jnp.min(jnp.where(c == m, f, 999.0), axis=0, keepdims=True)
            idx = i if idx is None else jnp.minimum(idx, i)
        for j, f in enumerate(flats):
            hit = f == idx
            counts[j] = counts[j] + hit.astype(F32)
            cands[j] = jnp.where(hit, NEG_BIG, cands[j])
        z = z + jnp.exp(m - cmax)
    n = [c.sum(axis=0, keepdims=True) for c in counts[:8]]
    n += [counts[8][r:r + 1] for r in range(8)]
    return n, z


def _peer_route_kernel(q_ref, keys_ref, r1_ref, cnt_ref, a_ref, b_ref):
    def head(h, carry):
        col = pl.multiple_of(h * (2 * PEER_HALF_DIM), 2 * PEER_HALF_DIM)
        q0 = q_ref[:, pl.ds(col, PEER_HALF_DIM)].astype(BF16)
        q1 = q_ref[:, pl.ds(col + PEER_HALF_DIM, PEER_HALF_DIM)].astype(BF16)
        s0 = lax.dot_general(keys_ref[2 * h], q0, NT_DIMS, preferred_element_type=F32)
        s1 = lax.dot_general(keys_ref[2 * h + 1], q1, NT_DIMS, preferred_element_type=F32)
        v0, rank0 = _top16_rows(s0)
        v1, rank1 = _top16_rows(s1)
        n, z = _pair_counts(v0, v1)
        cnt = jnp.zeros(s0.shape, F32)
        for k0 in range(PEER_TOPK):
            cnt = jnp.where(rank0 == float(k0), n[k0], cnt)
        r1_ref[h] = rank1.astype(r1_ref.dtype)
        cnt_ref[h] = cnt
        a_ref[h] = jnp.exp(s0 - v0[0])
        b_ref[h] = (jnp.exp(s1 - v1[0]) / z).astype(b_ref.dtype)
        return carry

    lax.fori_loop(0, PEER_HEADS, head, 0)


def peer_route(q, keys, *, tm):
    n = q.shape[0]
    assert n % tm == 0
    sds = [jax.ShapeDtypeStruct((PEER_HEADS, PEER_KEYS, n), dt) for dt in (BF16, F32, F32, BF16)]
    ospec = pl.BlockSpec((PEER_HEADS, PEER_KEYS, tm), lambda i: (0, 0, i))
    return pl.pallas_call(
        _peer_route_kernel,
        grid=(n // tm,),
        in_specs=[
            pl.BlockSpec((tm, PEER_HEADS * 2 * PEER_HALF_DIM), lambda i: (i, 0)),
            pl.BlockSpec((2 * PEER_HEADS, PEER_KEYS, PEER_HALF_DIM), lambda i: (0, 0, 0)),
        ],
        out_specs=[ospec] * 4,
        out_shape=sds,
        compiler_params=pltpu.CompilerParams(
            dimension_semantics=("parallel",), vmem_limit_bytes=VMEM_LIMIT_MID),
        name="peer_route",
    )(q, keys)


PEER_TE = 1024
PEER_ROWS = PEER_TE // PEER_KEYS


PEER_CHUNK = 16


def _peer_gate_block(act_ref, p_ref, r1_ref, b_ref, cnt_ref, a_ref, r, tc):
    cols = slice(tc * 128, (tc + 1) * 128)
    n_chunks = PEER_KEYS // PEER_CHUNK
    w = [jnp.zeros((PEER_CHUNK, 128), BF16) for _ in range(n_chunks)]
    zero = jnp.zeros((), BF16)
    for h in range(PEER_HEADS):
        cnt = jnp.broadcast_to(cnt_ref[h, 0, r:r + 1, cols], (PEER_CHUNK, 128)).astype(BF16)
        a = jnp.broadcast_to(a_ref[h, 0, r:r + 1, cols], (PEER_CHUNK, 128)).astype(BF16)
        for c in range(n_chunks):
            rows = slice(c * PEER_CHUNK, (c + 1) * PEER_CHUNK)
            first = jnp.minimum(jnp.maximum(cnt - r1_ref[h, rows, cols], zero), a)
            w[c] = w[c] + first * b_ref[h, rows, cols]
    for c in range(n_chunks):
        rows = slice(r * PEER_KEYS + c * PEER_CHUNK, r * PEER_KEYS + (c + 1) * PEER_CHUNK)
        g = _gelu_tanh(act_ref[rows, cols])
        p_ref[rows, cols] = (w[c].astype(F32) * g).astype(BF16)


def _peer_main_kernel(ht_ref, r1_ref, b_ref, cnt_ref, a_ref, u_ref, vt_ref, x_ref, o_ref,
                      acc_ref, act0_ref, act1_ref, p0_ref, p1_ref, *, tm, n_e):
    s = pl.program_id(1)
    acts = (act0_ref, act1_ref)
    ps = (p0_ref, p1_ref)

    def act_piece(parity, mh, nh):
        rows = slice(mh * (PEER_TE // 2), (mh + 1) * (PEER_TE // 2))
        cols = slice(nh * (tm // 2), (nh + 1) * (tm // 2))
        acts[parity][rows, cols] = jnp.dot(u_ref[rows, :], ht_ref[:, cols], preferred_element_type=F32)

    def out_piece(parity, mh, nh):
        d = acc_ref.shape[0]
        rows = slice(mh * (d // 2), (mh + 1) * (d // 2))
        cols = slice(nh * (tm // 2), (nh + 1) * (tm // 2))
        acc_ref[rows, cols] += jnp.dot(vt_ref[rows, :], ps[parity][:, cols], preferred_element_type=F32)

    def step(parity, do_act, do_gates, do_out):
        halves = [(mh, nh) for mh in range(2) for nh in range(2)]
        pieces = [functools.partial(act_piece, parity, *h) for h in halves] if do_act else []
        pieces += [functools.partial(out_piece, parity, *h) for h in halves] if do_out else []
        blocks = [(r, tc) for r in range(PEER_ROWS) for tc in range(tm // 128)] if do_gates else []
        per = -(-len(blocks) // max(len(pieces), 1))
        for i, piece in enumerate(pieces):
            piece()
            for r, tc in blocks[i * per:(i + 1) * per]:
                _peer_gate_block(acts[1 - parity], ps[1 - parity], r1_ref, b_ref, cnt_ref, a_ref, r, tc)
        for r, tc in blocks[len(pieces) * per:]:
            _peer_gate_block(acts[1 - parity], ps[1 - parity], r1_ref, b_ref, cnt_ref, a_ref, r, tc)

    @pl.when(s == 0)
    def _():
        acc_ref[...] = jnp.zeros_like(acc_ref)
        step(0, True, False, False)

    @pl.when(s == 1)
    def _():
        step(1, True, True, False)

    for parity in range(2):
        @pl.when(jnp.logical_and(jnp.logical_and(s >= 2, s < n_e), s % 2 == parity))
        def _():
            step(parity, True, True, True)

    @pl.when(s == n_e)
    def _():
        step(n_e % 2, False, True, True)

    @pl.when(s == n_e + 1)
    def _():
        step((n_e + 1) % 2, False, False, True)
        o_ref[...] = x_ref[...] + acc_ref[...].T


def peer_main(ht, route, u_all, vt_all, layer, x, *, tm):
    d, n = ht.shape
    rank1, cnt0, a, b = route
    n_e = PEER_EXPERTS // PEER_TE
    rows_shape = (PEER_HEADS, n_e, PEER_ROWS, n)
    clamp = lambda v: jnp.clip(v, 0, n_e - 1)
    full_spec = pl.BlockSpec((PEER_HEADS, PEER_KEYS, tm), lambda i, s: (0, 0, i))
    rows_spec = pl.BlockSpec((PEER_HEADS, 1, PEER_ROWS, tm), lambda i, s: (0, clamp(s - 1), 0, i))
    return pl.pallas_call(
        functools.partial(_peer_main_kernel, tm=tm, n_e=n_e),
        grid=(n // tm, n_e + 2),
        in_specs=[
            pl.BlockSpec((d, tm), lambda i, s: (0, i)),
            full_spec, full_spec, rows_spec, rows_spec,
            pl.BlockSpec((None, PEER_TE, d), lambda i, s: (layer, clamp(s), 0)),
            pl.BlockSpec((None, d, PEER_TE), lambda i, s: (layer, 0, clamp(s - 2))),
            pl.BlockSpec((tm, d), lambda i, s: (i, 0)),
        ],
        out_specs=pl.BlockSpec((tm, d), lambda i, s: (i, 0)),
        out_shape=jax.ShapeDtypeStruct((n, d), F32),
        scratch_shapes=[
            pltpu.VMEM((d, tm), F32),
            pltpu.VMEM((PEER_TE, tm), F32),
            pltpu.VMEM((PEER_TE, tm), F32),
            pltpu.VMEM((PEER_TE, tm), BF16),
            pltpu.VMEM((PEER_TE, tm), BF16),
        ],
        compiler_params=pltpu.CompilerParams(
            dimension_semantics=("parallel", "arbitrary"), vmem_limit_bytes=VMEM_LIMIT_BIG),
        name="peer_main",
    )(ht, rank1, b, cnt0.reshape(rows_shape), a.reshape(rows_shape), u_all, vt_all, x)


def peer_layer(x, g, w_q, keys, u_all, vt_all, layer, *, tm_q=512, tm_route=256, tm_main=512):
    q, ht = rms_matmul(x, g, w_q, tm=tm_q, tn=1024, emit_ht=True)
    route = peer_route(q, keys, tm=tm_route)
    return peer_main(ht, route, u_all, vt_all, layer, x, tm=tm_main)


CONV_TM = 256
CONV_TAIL = 8


def _conv_core_kernel(b_ref, c_ref, h_ref, x_ref, s0_ref, s1_ref, cw_ref, wout_ref,
                      o_ref, tail_ref, us_ref, ubuf, *, prompt_tiles, tiles_per_seq):
    i = pl.program_id(0)
    tm = CONV_TM
    u = c_ref[...] * h_ref[...]
    w0 = cw_ref[0:1, :]
    w1 = cw_ref[1:2, :]
    w2 = cw_ref[2:3, :]

    @pl.when(i % tiles_per_seq == 0)
    def _():
        ubuf[0:CONV_TAIL, :] = jnp.zeros((CONV_TAIL, D_MODEL), F32)

    ubuf[CONV_TAIL:CONV_TAIL + tm, :] = u
    tail_ref[0] = u[tm - CONV_TAIL:, :]

    @pl.when(i < prompt_tiles)
    def _():
        z = ubuf[CONV_TAIL - 2:CONV_TAIL - 2 + tm, :] * w0 + ubuf[CONV_TAIL - 1:CONV_TAIL - 1 + tm, :] * w1 + u * w2
        y = jnp.dot((b_ref[...] * z).astype(BF16), wout_ref[...], preferred_element_type=F32)
        o_ref[...] = x_ref[...] + y
        ubuf[0:CONV_TAIL, :] = u[tm - CONV_TAIL:, :]

    @pl.when(i >= prompt_tiles)
    def _():
        z = s0_ref[...] * w0 + s1_ref[...] * w1 + u * w2
        y = jnp.dot((b_ref[...] * z).astype(BF16), wout_ref[...], preferred_element_type=F32)
        o_ref[...] = x_ref[...] + y
        us_ref[...] = u


def conv_core(p, x, s0, s1, cw, wout, *, n_prompt, seq_len):
    n, d = x.shape
    tm = CONV_TM
    n_tiles = n // tm
    prompt_tiles = n_prompt // tm
    tiles_per_seq = seq_len // tm
    n_dec = n - n_prompt
    dec_map = lambda i: (jnp.maximum(i - prompt_tiles, 0), 0)
    return pl.pallas_call(
        functools.partial(_conv_core_kernel, prompt_tiles=prompt_tiles, tiles_per_seq=tiles_per_seq),
        grid=(n_tiles,),
        in_specs=[
            pl.BlockSpec((tm, d), lambda i: (i, 0)),
            pl.BlockSpec((tm, d), lambda i: (i, 1)),
            pl.BlockSpec((tm, d), lambda i: (i, 2)),
            pl.BlockSpec((tm, d), lambda i: (i, 0)),
            pl.BlockSpec((tm, d), dec_map),
            pl.BlockSpec((tm, d), dec_map),
            pl.BlockSpec((3, d), lambda i: (0, 0)),
            pl.BlockSpec((d, d), lambda i: (0, 0)),
        ],
        out_specs=[
            pl.BlockSpec((tm, d), lambda i: (i, 0)),
            pl.BlockSpec((1, CONV_TAIL, d), lambda i: (i, 0, 0)),
            pl.BlockSpec((tm, d), dec_map),
        ],
        out_shape=[
            jax.ShapeDtypeStruct((n, d), F32),
            jax.ShapeDtypeStruct((n_tiles, CONV_TAIL, d), F32),
            jax.ShapeDtypeStruct((n_dec, d), F32),
        ],
        scratch_shapes=[pltpu.VMEM((CONV_TAIL + tm, d), F32)],
        compiler_params=pltpu.CompilerParams(
            dimension_semantics=("arbitrary",), vmem_limit_bytes=VMEM_LIMIT_BIG),
        name="conv_core",
    )(p, p, p, x, s0, s1, cw, wout)


N_HEADS = 16
N_KV_HEADS = 4
HPG = N_HEADS // N_KV_HEADS
HEAD_DIM = 128
Q_DIM = N_HEADS * HEAD_DIM
KV_DIM = N_KV_HEADS * HEAD_DIM
CMP_BLOCK = 64
N_SELECT = 16
WINDOW = 512
ROPE_THETA = 10000.0
NEG_INF = -1e30
FORCED_SCORE = 1e6


def _rope(x, cos, sin_signed):
    return x * cos + pltpu.roll(x, HEAD_DIM // 2, 1) * sin_signed


def _nsa_layout_kernel(p_ref, cos_ref, sin_ref, q_ref, rows_ref, win_ref):
    cos = cos_ref[...]
    sin = sin_ref[...]
    scale = HEAD_DIM ** -0.5
    for h in range(N_HEADS):
        q_ref[0, h] = (_rope(p_ref[:, h * HEAD_DIM:(h + 1) * HEAD_DIM], cos, sin) * scale).astype(BF16)
    for c in range(6):
        for g in range(N_KV_HEADS):
            col = Q_DIM + c * KV_DIM + g * HEAD_DIM
            v = p_ref[:, col:col + HEAD_DIM]
            if c % 2 == 0:
                v = _rope(v, cos, sin)
            if c < 4:
                rows_ref[0, 0, c, g] = v
            else:
                win_ref[0, c - 4, g] = v


def nsa_layout(p, cos, sin, *, n_seq, seq_len, row0, tq):
    width = Q_DIM + 6 * KV_DIM
    n_t = seq_len // tq
    blk0 = row0 // tq
    assert row0 % tq == 0
    return pl.pallas_call(
        _nsa_layout_kernel,
        grid=(n_seq, n_t),
        in_specs=[
            pl.BlockSpec((tq, width), lambda b, t: (blk0 + b * n_t + t, 0)),
            pl.BlockSpec((tq, HEAD_DIM), lambda b, t: (t, 0)),
            pl.BlockSpec((tq, HEAD_DIM), lambda b, t: (t, 0)),
        ],
        out_specs=[
            pl.BlockSpec((1, N_HEADS, tq, HEAD_DIM), lambda b, t: (b, 0, t, 0)),
            pl.BlockSpec((1, 1, 4, N_KV_HEADS, tq, HEAD_DIM), lambda b, t: (b, 0, 0, 0, t, 0)),
            pl.BlockSpec((1, 2, N_KV_HEADS, tq, HEAD_DIM), lambda b, t: (b, 0, 0, t, 0)),
        ],
        out_shape=[
            jax.ShapeDtypeStruct((n_seq, N_HEADS, seq_len, HEAD_DIM), BF16),
            jax.ShapeDtypeStruct((n_seq, 1, 4, N_KV_HEADS, seq_len, HEAD_DIM), F32),
            jax.ShapeDtypeStruct((n_seq, 2, N_KV_HEADS, seq_len, HEAD_DIM), F32),
        ],
        compiler_params=pltpu.CompilerParams(
            dimension_semantics=("parallel", "parallel"), vmem_limit_bytes=VMEM_LIMIT_MID),
        name="nsa_layout",
    )(p, cos, sin)


def rope_tables(pos):
    half = HEAD_DIM // 2
    inv = ROPE_THETA ** (-jnp.arange(half, dtype=F32) / half)
    ang = pos.astype(F32)[:, None] * inv[None, :]
    cos = jnp.cos(ang)
    sin = jnp.sin(ang)
    return jnp.concatenate([cos, cos], axis=1), jnp.concatenate([-sin, sin], axis=1)


CMP_HIDDEN = 256
CMP_PAIR = 2
CMP_PASSES = CMP_BLOCK // CMP_PAIR


def _compress_rows(position_rows, n_rows, pe_ref, w1_ref, w2_ref):
    acc = jnp.zeros((n_rows, CMP_HIDDEN), F32)
    for p in range(CMP_PASSES):
        pair = [position_rows(CMP_PAIR * p + j) + pe_ref[CMP_PAIR * p + j:CMP_PAIR * p + j + 1, :]
                for j in range(CMP_PAIR)]
        kb = jnp.concatenate(pair, axis=1).astype(BF16)
        acc = acc + jnp.dot(kb, w1_ref[p], preferred_element_type=F32)
    hid = _gelu_tanh(acc)
    return jnp.dot(hid.astype(BF16), w2_ref[...], preferred_element_type=F32)


def _compress_kernel(blk_ref, x_ref, pe_ref, w1_ref, w2_ref, o_ref):
    del blk_ref
    o_ref[...] = _compress_rows(lambda l: x_ref[:, l, :], x_ref.shape[0], pe_ref, w1_ref, w2_ref)


def _compress_weight_specs():
    const = lambda n: (lambda *args: (0,) * n)
    return [
        pl.BlockSpec((CMP_BLOCK, HEAD_DIM), const(2)),
        pl.BlockSpec((CMP_PASSES, CMP_PAIR * HEAD_DIM, CMP_HIDDEN), const(3)),
        pl.BlockSpec((CMP_HIDDEN, HEAD_DIM), const(2)),
    ]


def compress_blocks(x, pe, w1, w2, *, tr, blocks):
    blocks = jnp.asarray(blocks, jnp.int32)
    nb = blocks.shape[0]
    return pl.pallas_call(
        _compress_kernel,
        grid_spec=pltpu.PrefetchScalarGridSpec(
            num_scalar_prefetch=1,
            grid=(nb,),
            in_specs=[pl.BlockSpec((tr, CMP_BLOCK, HEAD_DIM), lambda i, blk: (blk[i], 0, 0))]
            + _compress_weight_specs(),
            out_specs=pl.BlockSpec((tr, HEAD_DIM), lambda i, blk: (i, 0)),
        ),
        out_shape=jax.ShapeDtypeStruct((nb * tr, HEAD_DIM), F32),
        compiler_params=pltpu.CompilerParams(
            dimension_semantics=("arbitrary",), vmem_limit_bytes=VMEM_LIMIT_MID),
        name="compress_blocks",
    )(blocks, x, pe, w1, w2)


ATT_TQ = 256
ATT_TK = 512


def _softmax_masked(s, mask, axis):
    sm = jnp.where(mask, s, NEG_INF)
    e = jnp.exp(sm - jnp.max(sm, axis=axis, keepdims=True))
    p = e / jnp.sum(e, axis=axis, keepdims=True)
    return jnp.where(mask, p, 0.0)


def _block_ranks(score_t, n_real):
    blk = lax.broadcasted_iota(jnp.int32, score_t.shape, 0)
    rank = jnp.zeros(score_t.shape, F32)
    for m in range(n_real):
        row = score_t[m:m + 1, :]
        earlier = jnp.where(blk > m, 1.0, 0.0)
        rank = rank + jnp.where(row > score_t, 1.0, jnp.where(row == score_t, earlier, 0.0))
    return rank


def _select_blocks(score_t):
    return jnp.where(_block_ranks(score_t, score_t.shape[0]) < float(N_SELECT), 1.0, 0.0)


def _flash(q_ref, k_tile, v_tile, kt_lo, kt_hi, bias_of, m_scr, l_scr, acc_scr):
    m_scr[...] = jnp.full(m_scr.shape, NEG_INF, F32)
    l_scr[...] = jnp.zeros(l_scr.shape, F32)
    acc_scr[...] = jnp.zeros(acc_scr.shape, F32)
    q2 = q_ref[0].reshape(HPG * ATT_TQ, HEAD_DIM)

    def body(kt, carry):
        s = lax.dot_general(q2, k_tile(kt), NT_DIMS, preferred_element_type=F32).reshape(HPG, ATT_TQ, ATT_TK)
        s = s + bias_of(kt)[None]
        m_old = m_scr[...]
        m_new = jnp.maximum(m_old, jnp.max(s, axis=-1, keepdims=True))
        alpha = jnp.exp(m_old - m_new)
        p = jnp.exp(s - m_new)
        l_scr[...] = alpha * l_scr[...] + jnp.sum(p, axis=-1, keepdims=True)
        pv = jnp.dot(p.reshape(HPG * ATT_TQ, ATT_TK).astype(BF16), v_tile(kt), preferred_element_type=F32)
        acc_scr[...] = alpha * acc_scr[...] + pv.reshape(HPG, ATT_TQ, HEAD_DIM)
        m_scr[...] = m_new
        return carry

    lax.fori_loop(kt_lo, kt_hi, body, 0)
    return acc_scr[...] / l_scr[...]


def _nsa_prompt_attn_kernel(q_ref, kc_ref, vc_ref, ks_ref, vs_ref, kw_ref, vw_ref, gate_ref, o_ref,
                            m_scr, l_scr, acc_scr):
    tq = ATT_TQ
    qi = pl.program_id(2)
    q0 = qi * tq
    q3 = q_ref[0]
    q2 = q3.reshape(HPG * tq, HEAD_DIM)
    nb = kc_ref.shape[2]
    kc = kc_ref[0, 0].astype(BF16)
    vc = vc_ref[0, 0].astype(BF16)

    pos_col = q0 + lax.broadcasted_iota(jnp.int32, (tq, 1), 0)
    blk_row = lax.broadcasted_iota(jnp.int32, (1, nb), 1)
    c_mask = ((blk_row + 1) * CMP_BLOCK - 1) <= pos_col
    s_c = lax.dot_general(q2, kc, NT_DIMS, preferred_element_type=F32).reshape(HPG, tq, nb)
    p_c = _softmax_masked(s_c, c_mask[None], axis=-1)
    o_c = jnp.dot(p_c.reshape(HPG * tq, nb).astype(BF16), vc, preferred_element_type=F32).reshape(HPG, tq, HEAD_DIM)

    pos_row = q0 + lax.broadcasted_iota(jnp.int32, (1, tq), 1)
    blk_col = lax.broadcasted_iota(jnp.int32, (nb, 1), 0)
    c_mask_t = ((blk_col + 1) * CMP_BLOCK - 1) <= pos_row
    imp = jnp.zeros((nb, tq), F32)
    for h in range(HPG):
        s_t = lax.dot_general(kc, q3[h], NT_DIMS, preferred_element_type=F32)
        imp = imp + _softmax_masked(s_t, c_mask_t, axis=0)
    cur = pos_row // CMP_BLOCK
    reach = blk_col <= cur
    forced = jnp.logical_and(reach, jnp.logical_or(blk_col == 0, blk_col >= cur - 1))
    score = jnp.where(forced, FORCED_SCORE, jnp.where(reach, imp, NEG_INF))
    sel_t = _select_blocks(score).astype(BF16)
    eye = jnp.where(lax.broadcasted_iota(jnp.int32, (tq, tq), 0) == lax.broadcasted_iota(jnp.int32, (tq, tq), 1),
                    1.0, 0.0).astype(BF16)
    sel = lax.dot_general(eye, sel_t, NT_DIMS, preferred_element_type=F32).astype(BF16)

    def key_pos(kt):
        return kt * ATT_TK + lax.broadcasted_iota(jnp.int32, (1, ATT_TK), 1)

    def tile_of(ref, lead):
        def get(kt):
            return ref[lead + (pl.ds(pl.multiple_of(kt * ATT_TK, ATT_TK), ATT_TK), slice(None))].astype(BF16)
        return get

    def sel_bias(kt):
        kpos = key_pos(kt)
        expand = jnp.where((kpos // CMP_BLOCK) == blk_col, 1.0, 0.0).astype(BF16)
        in_sel = jnp.dot(sel, expand, preferred_element_type=F32) > 0.5
        return jnp.where(jnp.logical_and(in_sel, kpos <= pos_col), 0.0, NEG_INF)

    def win_bias(kt):
        dist = pos_col - key_pos(kt)
        return jnp.where(jnp.logical_and(dist >= 0, dist <= WINDOW), 0.0, NEG_INF)

    scratch = (m_scr, l_scr, acc_scr)
    kt_hi = (q0 + tq + ATT_TK - 1) // ATT_TK
    lead_s = (0, 0, 0, 0)
    o_s = _flash(q_ref, tile_of(ks_ref, lead_s), tile_of(vs_ref, lead_s), 0, kt_hi, sel_bias, *scratch)
    lead_w = (0, 0, 0)
    kt_lo = jnp.maximum(q0 - WINDOW, 0) // ATT_TK
    o_w = _flash(q_ref, tile_of(kw_ref, lead_w), tile_of(vw_ref, lead_w), kt_lo, kt_hi, win_bias, *scratch)

    gate = jax.nn.sigmoid(gate_ref[0, 0])
    for h in range(HPG):
        o = (gate[:, 3 * h:3 * h + 1] * o_c[h] + gate[:, 3 * h + 1:3 * h + 2] * o_s[h]
             + gate[:, 3 * h + 2:3 * h + 3] * o_w[h])
        o_ref[:, h * HEAD_DIM:(h + 1) * HEAD_DIM] = o


def nsa_prompt_attn(q, kc, vc, rows, win, gates):
    b, _, t, _ = q.shape
    nb = kc.shape[2]
    n_q = t // ATT_TQ
    kv_spec = lambda comp: pl.BlockSpec((1, 1, 1, 1, t, HEAD_DIM), lambda bi, g, qi: (bi, 0, comp, g, 0, 0))
    win_spec = lambda comp: pl.BlockSpec((1, 1, 1, t, HEAD_DIM), lambda bi, g, qi: (bi, comp, g, 0, 0))
    cmp_spec = pl.BlockSpec((1, 1, nb, HEAD_DIM), lambda bi, g, qi: (bi, g, 0, 0))
    return pl.pallas_call(
        _nsa_prompt_attn_kernel,
        grid=(b, N_KV_HEADS, n_q),
        in_specs=[
            pl.BlockSpec((1, HPG, ATT_TQ, HEAD_DIM), lambda bi, g, qi: (bi, g, qi, 0)),
            cmp_spec, cmp_spec,
            kv_spec(2), kv_spec(3),
            win_spec(0), win_spec(1),
            pl.BlockSpec((1, 1, ATT_TQ, 3 * HPG), lambda bi, g, qi: (bi, g, qi, 0)),
        ],
        out_specs=pl.BlockSpec((ATT_TQ, HPG * HEAD_DIM), lambda bi, g, qi: (bi * n_q + qi, g)),
        out_shape=jax.ShapeDtypeStruct((b * t, Q_DIM), F32),
        scratch_shapes=[
            pltpu.VMEM((HPG, ATT_TQ, 1), F32),
            pltpu.VMEM((HPG, ATT_TQ, 1), F32),
            pltpu.VMEM((HPG, ATT_TQ, HEAD_DIM), F32),
        ],
        compiler_params=pltpu.CompilerParams(
            dimension_semantics=("parallel", "parallel", "arbitrary"), vmem_limit_bytes=VMEM_LIMIT_MID),
        name="nsa_prompt_attn",
    )(q, kc, vc, rows, rows, win, win, gates)


PAGE_SIZE = 128
DEC_ROWS = 512
DEC_NB_PAD = 128
NSA_PROJ_PAD = 5632
BLOCKS_PER_PAGE = PAGE_SIZE // CMP_BLOCK
PAGE_ROWS = N_KV_HEADS * BLOCKS_PER_PAGE
CMP_PAGES_PER_STEP = 32


def _paged_compress_kernel(pt_ref, cache_ref, pe_ref, w1_ref, w2_ref, o_ref, buf, sem, *, comp):
    s = pl.program_id(0)
    n_steps = pl.num_programs(0)

    def block_copy(step, p, row):
        slot = step % 2
        page = pt_ref[step * CMP_PAGES_PER_STEP + p]
        return pltpu.make_async_copy(cache_ref.at[page, comp, row], buf.at[slot, :, p * PAGE_ROWS + row, :],
                                     sem.at[slot])

    def for_all_blocks(step, fn):
        for p in range(CMP_PAGES_PER_STEP):
            for row in range(PAGE_ROWS):
                fn(block_copy(step, p, row))

    @pl.when(s == 0)
    def _():
        for_all_blocks(0, lambda cp: cp.start())

    @pl.when(s + 1 < n_steps)
    def _():
        for_all_blocks(s + 1, lambda cp: cp.start())

    for_all_blocks(s, lambda cp: cp.wait())
    slot = s % 2
    n_rows = CMP_PAGES_PER_STEP * PAGE_ROWS
    o_ref[...] = _compress_rows(lambda l: buf[slot, l], n_rows, pe_ref, w1_ref, w2_ref)


def paged_compress(page_table_flat, cache, pe, w1, w2, *, comp):
    n_pages = page_table_flat.shape[0]
    assert n_pages % CMP_PAGES_PER_STEP == 0
    rows = CMP_PAGES_PER_STEP * PAGE_ROWS
    return pl.pallas_call(
        functools.partial(_paged_compress_kernel, comp=comp),
        grid_spec=pltpu.PrefetchScalarGridSpec(
            num_scalar_prefetch=1,
            grid=(n_pages // CMP_PAGES_PER_STEP,),
            in_specs=[pl.BlockSpec(memory_space=pl.ANY)] + _compress_weight_specs(),
            out_specs=pl.BlockSpec((rows, HEAD_DIM), lambda i, pt: (i, 0)),
            scratch_shapes=[
                pltpu.VMEM((2, CMP_BLOCK, rows, HEAD_DIM), F32),
                pltpu.SemaphoreType.DMA((2,)),
            ],
        ),
        out_shape=jax.ShapeDtypeStruct((n_pages * PAGE_ROWS, HEAD_DIM), F32),
        compiler_params=pltpu.CompilerParams(
            dimension_semantics=("arbitrary",), vmem_limit_bytes=VMEM_LIMIT_MID),
        name="paged_compress",
    )(page_table_flat, cache, pe, w1, w2)


def _group_diag(rows_per_group, cols_per_group, shape):
    r = lax.broadcasted_iota(jnp.int32, shape, 0) // rows_per_group
    c = lax.broadcasted_iota(jnp.int32, shape, 1) // cols_per_group
    return r == c


def _decode_attn_a_kernel(q_ref, kc_ref, vc_ref, new_ref, gate_ref, win_ref,
                          ocw_ref, idx_ref, wout_ref, *, nb, nb_pad, pos):
    q = q_ref[0]
    kc = kc_ref[0].reshape(N_KV_HEADS * nb_pad, HEAD_DIM).astype(BF16)
    vc = vc_ref[0].reshape(N_KV_HEADS * nb_pad, HEAD_DIM).astype(BF16)
    gate = jax.nn.sigmoid(gate_ref[0])

    shape = (N_HEADS, N_KV_HEADS * nb_pad)
    blk = lax.broadcasted_iota(jnp.int32, shape, 1) % nb_pad
    complete = jnp.logical_and((blk + 1) * CMP_BLOCK - 1 <= pos, blk < nb)
    c_mask = jnp.logical_and(_group_diag(HPG, nb_pad, shape), complete)
    s_c = lax.dot_general(q, kc, NT_DIMS, preferred_element_type=F32)
    p_c = _softmax_masked(s_c, c_mask, axis=-1)
    o_c = jnp.dot(p_c.astype(BF16), vc, preferred_element_type=F32)

    blk_col = lax.broadcasted_iota(jnp.int32, (nb_pad, 1), 0)
    complete_t = jnp.logical_and((blk_col + 1) * CMP_BLOCK - 1 <= pos, blk_col < nb)
    lane16 = lax.broadcasted_iota(jnp.int32, (nb_pad, N_HEADS), 1)
    lane = lax.broadcasted_iota(jnp.int32, (nb_pad, 128), 1)
    imp = jnp.zeros((nb_pad, 128), F32)
    for g in range(N_KV_HEADS):
        s_t = lax.dot_general(kc[g * nb_pad:(g + 1) * nb_pad], q, NT_DIMS, preferred_element_type=F32)
        p_t = _softmax_masked(s_t, complete_t, axis=0)
        imp_g = jnp.sum(jnp.where(lane16 // HPG == g, p_t, 0.0), axis=1, keepdims=True)
        imp = jnp.where(lane == g, imp_g, imp)
    cur = pos // CMP_BLOCK
    reach = jnp.logical_and(blk_col <= cur, blk_col < nb)
    forced = jnp.logical_and(reach, jnp.logical_or(blk_col == 0, blk_col >= cur - 1))
    score = jnp.where(forced, FORCED_SCORE, jnp.where(reach, imp, NEG_INF))
    score = jnp.where(blk_col < nb, score, 2.0 * NEG_INF)
    rank = _block_ranks(score, nb)
    blk_f = blk_col.astype(F32)
    picked = [jnp.sum(jnp.where(rank == float(k), blk_f, 0.0), axis=0, keepdims=True) for k in range(N_SELECT)]
    idx_ref[0] = jnp.concatenate(picked, axis=0).astype(jnp.int32)

    wb = win_ref.shape[4]
    kw = win_ref[0, 0, 0].reshape(N_KV_HEADS * wb, HEAD_DIM).astype(BF16)
    vw = win_ref[0, 0, 1].reshape(N_KV_HEADS * wb, HEAD_DIM).astype(BF16)
    k_new = new_ref[0, 4]
    v_new = new_ref[0, 5]
    shape_w = (N_HEADS, N_KV_HEADS * wb)
    w_mask = _group_diag(HPG, wb, shape_w)
    s_w = jnp.where(w_mask, lax.dot_general(q, kw, NT_DIMS, preferred_element_type=F32), NEG_INF)
    s_n = jnp.sum(q.astype(F32) * k_new, axis=-1, keepdims=True)
    m = jnp.maximum(jnp.max(s_w, axis=-1, keepdims=True), s_n)
    e_w = jnp.where(w_mask, jnp.exp(s_w - m), 0.0)
    e_n = jnp.exp(s_n - m)
    denom = jnp.sum(e_w, axis=-1, keepdims=True) + e_n
    o_w = (jnp.dot(e_w.astype(BF16), vw, preferred_element_type=F32)
           + e_n * v_new) / denom

    ocw_ref[0] = gate[:, 0:1] * o_c + gate[:, 2:3] * o_w

    for c in range(2):
        for g in range(N_KV_HEADS):
            wout_ref[0, 0, c, g, 0:wb - 1, :] = win_ref[0, 0, c, g, 1:wb, :]
            wout_ref[0, 0, c, g, wb - 1:wb, :] = new_ref[0, 4 + c, g * HPG:g * HPG + 1, :]


def decode_attn_a(q, kc, vc, new_rows, gates, win, *, nb, pos):
    s = q.shape[0]
    nb_pad = kc.shape[2]
    wb = win.shape[4]
    cmp_spec = pl.BlockSpec((1, N_KV_HEADS, nb_pad, HEAD_DIM), lambda i: (i, 0, 0, 0))
    win_spec = pl.BlockSpec((1, 1, 2, N_KV_HEADS, wb, HEAD_DIM), lambda i: (i, 0, 0, 0, 0, 0))
    return pl.pallas_call(
        functools.partial(_decode_attn_a_kernel, nb=nb, nb_pad=nb_pad, pos=pos),
        grid=(s,),
        in_specs=[
            pl.BlockSpec((1, N_HEADS, HEAD_DIM), lambda i: (i, 0, 0)),
            cmp_spec, cmp_spec,
            pl.BlockSpec((1, 6, N_HEADS, HEAD_DIM), lambda i: (i, 0, 0, 0)),
            pl.BlockSpec((1, N_HEADS, 3), lambda i: (i, 0, 0)),
            win_spec,
        ],
        out_specs=[
            pl.BlockSpec((1, N_HEADS, HEAD_DIM), lambda i: (i, 0, 0)),
            pl.BlockSpec((1, N_SELECT, 128), lambda i: (i, 0, 0)),
            win_spec,
        ],
        out_shape=[
            jax.ShapeDtypeStruct((s, N_HEADS, HEAD_DIM), F32),
            jax.ShapeDtypeStruct((s, N_SELECT, 128), jnp.int32),
            jax.ShapeDtypeStruct(win.shape, F32),
        ],
        compiler_params=pltpu.CompilerParams(
            dimension_semantics=("parallel",), vmem_limit_bytes=VMEM_LIMIT_MID),
        name="decode_attn_a",
    )(q, kc, vc, new_rows, gates, win)


SEL_SLOTS = N_KV_HEADS * N_SELECT


def _decode_attn_b_kernel(pt_ref, idx_ref, q_ref, cache_ref, new_ref, gate_ref, ocw_ref, o_ref,
                          kbuf, vbuf, sem, *, pages_per_seq, pos):
    s = pl.program_id(0)
    nb_past = pages_per_seq * BLOCKS_PER_PAGE
    new_blk = pos // CMP_BLOCK
    blks = [idx_ref[s * SEL_SLOTS + slot] for slot in range(SEL_SLOTS)]

    def block_copies(slot):
        blk = jnp.minimum(blks[slot], nb_past - 1)
        page = pt_ref[s * pages_per_seq + blk // BLOCKS_PER_PAGE]
        row = (slot // N_SELECT) * BLOCKS_PER_PAGE + blk % BLOCKS_PER_PAGE
        return (pltpu.make_async_copy(cache_ref.at[page, 2, row], kbuf.at[slot], sem.at[0]),
                pltpu.make_async_copy(cache_ref.at[page, 3, row], vbuf.at[slot], sem.at[1]))

    for slot in range(SEL_SLOTS):
        for cp in block_copies(slot):
            cp.start()
    for slot in range(SEL_SLOTS):
        for cp in block_copies(slot):
            cp.wait()

    q = q_ref[0]
    n_keys = SEL_SLOTS * CMP_BLOCK
    shape = (N_HEADS, n_keys)
    lane = lax.broadcasted_iota(jnp.int32, (8, SEL_SLOTS), 1)
    valid = jnp.zeros((8, SEL_SLOTS), F32)
    for slot in range(SEL_SLOTS):
        valid = jnp.where(lane == slot, jnp.where(blks[slot] < nb_past, 1.0, 0.0), valid)
    expand = jnp.where(lax.broadcasted_iota(jnp.int32, (SEL_SLOTS, n_keys), 1) // CMP_BLOCK
                       == lax.broadcasted_iota(jnp.int32, (SEL_SLOTS, n_keys), 0), 1.0, 0.0).astype(BF16)
    valid_keys = jnp.dot(valid.astype(BF16), expand, preferred_element_type=F32)[0:1, :] > 0.5
    ok = jnp.logical_and(_group_diag(HPG, N_SELECT * CMP_BLOCK, shape), valid_keys)

    head_group = lax.broadcasted_iota(jnp.int32, (N_HEADS, 1), 0) // HPG
    sel_new = jnp.zeros((N_HEADS, 1), F32)
    for g in range(N_KV_HEADS):
        hit = blks[g * N_SELECT] == new_blk
        for k in range(1, N_SELECT):
            hit = jnp.logical_or(hit, blks[g * N_SELECT + k] == new_blk)
        sel_new = jnp.where(head_group == g, jnp.where(hit, 1.0, 0.0), sel_new)
    sel_new = sel_new > 0.5

    k_sel = kbuf[...].reshape(n_keys, HEAD_DIM).astype(BF16)
    v_sel = vbuf[...].reshape(n_keys, HEAD_DIM).astype(BF16)
    k_new = new_ref[0, 2]
    v_new = new_ref[0, 3]
    s_sel = jnp.where(ok, lax.dot_general(q, k_sel, NT_DIMS, preferred_element_type=F32), NEG_INF)
    s_new = jnp.where(sel_new, jnp.sum(q.astype(F32) * k_new, axis=-1, keepdims=True), NEG_INF)
    m = jnp.maximum(jnp.max(s_sel, axis=-1, keepdims=True), s_new)
    e_sel = jnp.where(ok, jnp.exp(s_sel - m), 0.0)
    e_new = jnp.where(sel_new, jnp.exp(s_new - m), 0.0)
    l = jnp.sum(e_sel, axis=-1, keepdims=True) + e_new
    acc = jnp.dot(e_sel.astype(BF16), v_sel, preferred_element_type=F32) + e_new * v_new
    o_s = jnp.where(l > 0.0, acc / jnp.where(l > 0.0, l, 1.0), 0.0)
    gate = jax.nn.sigmoid(gate_ref[0])
    o_ref[0] = ocw_ref[0] + gate[:, 1:2] * o_s


def decode_attn_b(page_table_flat, idx_flat, q, cache, new_rows, gates, ocw, *, pages_per_seq, pos):
    s = q.shape[0]
    assert pos >= pages_per_seq * PAGE_SIZE - 1
    seq_spec = lambda shape: pl.BlockSpec((1,) + shape, lambda i, pt, idx: (i,) + (0,) * len(shape))
    return pl.pallas_call(
        functools.partial(_decode_attn_b_kernel, pages_per_seq=pages_per_seq, pos=pos),
        grid_spec=pltpu.PrefetchScalarGridSpec(
            num_scalar_prefetch=2,
            grid=(s,),
            in_specs=[
                seq_spec((N_HEADS, HEAD_DIM)),
                pl.BlockSpec(memory_space=pl.ANY),
                seq_spec((6, N_HEADS, HEAD_DIM)),
                seq_spec((N_HEADS, 3)),
                seq_spec((N_HEADS, HEAD_DIM)),
            ],
            out_specs=seq_spec((N_HEADS, HEAD_DIM)),
            scratch_shapes=[
                pltpu.VMEM((SEL_SLOTS, CMP_BLOCK, HEAD_DIM), F32),
                pltpu.VMEM((SEL_SLOTS, CMP_BLOCK, HEAD_DIM), F32),
                pltpu.SemaphoreType.DMA((2,)),
            ],
        ),
        out_shape=jax.ShapeDtypeStruct((s, N_HEADS, HEAD_DIM), F32),
        compiler_params=pltpu.CompilerParams(
            dimension_semantics=("arbitrary",), vmem_limit_bytes=VMEM_LIMIT_MID),
        name="decode_attn_b",
    )(page_table_flat, idx_flat, q, cache, new_rows, gates, ocw)


def nsa_decode_group(pn, row0, s, cache_nsa_kv, state_nsa_win, page_table, cmp_k, cmp_v):
    pages_per_seq = page_table.shape[1]
    past = pages_per_seq * PAGE_SIZE
    nb_dec = past // CMP_BLOCK + 1
    proj = Q_DIM + 6 * KV_DIM
    cos_s, sin_s = rope_tables(jnp.full((s,), past, jnp.int32))
    q_s, rows_s, win_s = nsa_layout(pn, cos_s, sin_s, n_seq=1, seq_len=s, row0=row0, tq=s)
    q_s = q_s[0].transpose(1, 0, 2)
    new_rows = jnp.concatenate([rows_s[0, 0], win_s[0]], axis=0).transpose(2, 0, 1, 3)
    new16 = jnp.repeat(new_rows, HPG, axis=2)
    pt_flat = page_table.reshape(-1)
    cache5 = cache_nsa_kv.reshape(cache_nsa_kv.shape[0], 4, PAGE_ROWS, CMP_BLOCK, HEAD_DIM)

    def summaries(comp, pe, w1, w2):
        past_c = paged_compress(pt_flat, cache5, pe, w1, w2, comp=comp)
        past_c = past_c.reshape(s, pages_per_seq, N_KV_HEADS, BLOCKS_PER_PAGE, HEAD_DIM)
        past_c = past_c.transpose(0, 2, 1, 3, 4).reshape(s, N_KV_HEADS, pages_per_seq * BLOCKS_PER_PAGE, HEAD_DIM)
        new_blk = jnp.pad(new_rows[:, comp].reshape(s * N_KV_HEADS, 1, HEAD_DIM), ((0, 0), (0, CMP_BLOCK - 1), (0, 0)))
        new_c = compress_blocks(new_blk, pe, w1, w2, tr=s * N_KV_HEADS, blocks=[0])
        both = jnp.concatenate([past_c, new_c.reshape(s, N_KV_HEADS, 1, HEAD_DIM)], axis=2)
        return jnp.pad(both, ((0, 0), (0, 0), (0, DEC_NB_PAD - nb_dec), (0, 0)))

    kc_s = summaries(0, *cmp_k)
    vc_s = summaries(1, *cmp_v)
    gates_s = pn[row0:row0 + s, proj:proj + 3 * N_HEADS].reshape(s, N_HEADS, 3)
    ocw, idx, win_out = decode_attn_a(q_s, kc_s, vc_s, new16, gates_s, state_nsa_win, nb=nb_dec, pos=past)
    idx_flat = idx[:, :, :N_KV_HEADS].transpose(0, 2, 1).reshape(-1)
    o_s = decode_attn_b(pt_flat, idx_flat, q_s, cache5, new16, gates_s, ocw, pages_per_seq=pages_per_seq, pos=past)
    return o_s.reshape(s, Q_DIM), new_rows[:, None, :4, :, None, :], win_out


def _matmul_residual_kernel(a_ref, a2_ref, w_ref, x_ref, o_ref, *, first_tiles):
    i = pl.program_id(0)

    @pl.when(i < first_tiles)
    def _():
        o_ref[...] = x_ref[...] + jnp.dot(a_ref[...].astype(BF16), w_ref[...], preferred_element_type=F32)

    @pl.when(i >= first_tiles)
    def _():
        o_ref[...] = x_ref[...] + jnp.dot(a2_ref[...].astype(BF16), w_ref[...], preferred_element_type=F32)


def matmul_residual(a, a2, w, x, *, tm):
    n1, k = a.shape
    n2 = a2.shape[0]
    n, d = x.shape
    assert n == n1 + n2 and n1 % tm == 0 and n2 % tm == 0
    first = n1 // tm
    return pl.pallas_call(
        functools.partial(_matmul_residual_kernel, first_tiles=first),
        grid=(n // tm,),
        in_specs=[
            pl.BlockSpec((tm, k), lambda i: (jnp.minimum(i, first - 1), 0)),
            pl.BlockSpec((tm, k), lambda i: (jnp.maximum(i - first, 0), 0)),
            pl.BlockSpec((k, d), lambda i: (0, 0)),
            pl.BlockSpec((tm, d), lambda i: (i, 0)),
        ],
        out_specs=pl.BlockSpec((tm, d), lambda i: (i, 0)),
        out_shape=jax.ShapeDtypeStruct((n, d), F32),
        compiler_params=pltpu.CompilerParams(
            dimension_semantics=("parallel",), vmem_limit_bytes=VMEM_LIMIT_MID),
        name="matmul_residual",
    )(a, a2, w, x)


def _rmsnorm_kernel(x_ref, g_ref, o_ref):
    x = x_ref[...]
    ms = jnp.mean(x * x, axis=-1, keepdims=True)
    o_ref[...] = x * lax.rsqrt(ms + RMS_EPS) * g_ref[...]


def rmsnorm_rows(x, g, *, row0, n_rows, tm):
    d = x.shape[1]
    assert row0 % tm == 0 and n_rows % tm == 0
    blk0 = row0 // tm
    return pl.pallas_call(
        _rmsnorm_kernel,
        grid=(n_rows // tm,),
        in_specs=[
            pl.BlockSpec((tm, d), lambda i: (blk0 + i, 0)),
            pl.BlockSpec((1, d), lambda i: (0, 0)),
        ],
        out_specs=pl.BlockSpec((tm, d), lambda i: (i, 0)),
        out_shape=jax.ShapeDtypeStruct((n_rows, d), F32),
        compiler_params=pltpu.CompilerParams(dimension_semantics=("parallel",)),
        name="rmsnorm_rows",
    )(x, g.reshape(1, d))


def kernel(x_prompt, x_sample, state_conv, cache_nsa_kv, state_nsa_win, page_table, norm_mix, norm_ffn, norm_final, conv_w_in, conv_w, conv_w_out, nsa_w_in, nsa_w_out, nsa_cmp_pe_k, nsa_cmp_w1_k, nsa_cmp_w2_k, nsa_cmp_pe_v, nsa_cmp_w1_v, nsa_cmp_w2_v, peer_w_q, peer_sub_keys, peer_u, peer_v):
    b, t, d = x_prompt.shape
    s = x_sample.shape[0]
    assert x_sample.shape[1] == 1 and d == D_MODEL
    n_prompt = b * t
    n_dec = DEC_ROWS

    u_all = peer_u.astype(BF16)
    vt_all = peer_v.astype(BF16).transpose(0, 2, 1)

    def peer(x, i):
        return peer_layer(
            x, norm_ffn[i], peer_w_q[i].astype(BF16),
            peer_sub_keys[i].reshape(2 * PEER_HEADS, PEER_KEYS, PEER_HALF_DIM).astype(BF16),
            u_all, vt_all, i)

    x = jnp.concatenate([x_prompt.reshape(n_prompt, d), x_sample.reshape(s, d),
                         jnp.zeros((n_dec - s, d), F32)], axis=0)

    p = rms_matmul(x, norm_mix[0], conv_w_in[0].astype(BF16), tm=512, tn=1024)
    st = state_conv[:, 0]
    pad_dec = lambda a: jnp.pad(a, ((0, n_dec - s), (0, 0)))
    x, tail, u_dec = conv_core(p, x, pad_dec(st[:, 0]), pad_dec(st[:, 1]), conv_w[0],
                               conv_w_out[0].astype(BF16), n_prompt=n_prompt, seq_len=t)
    tail_p = tail[:n_prompt // CONV_TM].reshape(b, t // CONV_TM, CONV_TAIL, d)
    conv_state_prompt = tail_p[:, -1, CONV_TAIL - 2:][:, None]
    conv_state_sample = jnp.stack([st[:, 1], u_dec[:s]], axis=1)[:, None]
    x = peer(x, 0)

    proj = Q_DIM + 6 * KV_DIM
    n_gate = 3 * N_HEADS
    w_in = jnp.pad(nsa_w_in[0], ((0, 0), (0, NSA_PROJ_PAD - proj - n_gate))).astype(BF16)
    pn = rms_matmul(x, norm_mix[1], w_in, tm=512, tn=512)
    gates = pn[:, proj:proj + n_gate]
    w1_shape = (CMP_PASSES, CMP_PAIR * HEAD_DIM, CMP_HIDDEN)
    w1k, w2k = nsa_cmp_w1_k[0].astype(BF16).reshape(w1_shape), nsa_cmp_w2_k[0].astype(BF16)
    w1v, w2v = nsa_cmp_w1_v[0].astype(BF16).reshape(w1_shape), nsa_cmp_w2_v[0].astype(BF16)
    pe_k, pe_v = nsa_cmp_pe_k[0], nsa_cmp_pe_v[0]

    cos_p, sin_p = rope_tables(jnp.arange(t))
    q_p, rows_p, win_p = nsa_layout(pn, cos_p, sin_p, n_seq=b, seq_len=t, row0=0, tq=ATT_TQ)
    nb = t // CMP_BLOCK
    flat = rows_p.reshape(b * 4 * N_KV_HEADS * nb, CMP_BLOCK, HEAD_DIM)
    tr = N_KV_HEADS * nb
    kc_p = compress_blocks(flat, pe_k, w1k, w2k, tr=tr, blocks=[4 * i for i in range(b)])
    vc_p = compress_blocks(flat, pe_v, w1v, w2v, tr=tr, blocks=[4 * i + 1 for i in range(b)])
    gates_p = gates[:n_prompt].reshape(b, t, N_KV_HEADS, 3 * HPG).transpose(0, 2, 1, 3)
    o_p = nsa_prompt_attn(q_p, kc_p.reshape(b, N_KV_HEADS, nb, HEAD_DIM), vc_p.reshape(b, N_KV_HEADS, nb, HEAD_DIM),
                          rows_p, win_p, gates_p)
    wb_p = min(WINDOW, t)
    nsa_win_prompt = win_p[:, None, :, :, t - wb_p:, :]

    o_s, nsa_rows_sample, nsa_win_sample = nsa_decode_group(
        pn, n_prompt, s, cache_nsa_kv, state_nsa_win, page_table, (pe_k, w1k, w2k), (pe_v, w1v, w2v))
    o_s = jnp.pad(o_s, ((0, n_dec - s), (0, 0)))

    x = matmul_residual(o_p, o_s, nsa_w_out[0].astype(BF16), x, tm=256)
    x = peer(x, 1)

    y_prompt = rmsnorm_rows(x, norm_final, row0=0, n_rows=n_prompt, tm=512).reshape(b, t, d)
    y_sample = rmsnorm_rows(x, norm_final, row0=n_prompt, n_rows=s, tm=s).reshape(s, 1, d)
    return (y_prompt, y_sample, conv_state_prompt, conv_state_sample,
            rows_p, nsa_rows_sample, nsa_win_prompt, nsa_win_sample)
```

```python
import functools
import math

import jax
import jax.numpy as jnp
from jax import lax
from jax.experimental import pallas as pl
from jax.experimental.pallas import tpu as pltpu

F32 = jnp.float32
BF16 = jnp.bfloat16

D_MODEL = 2048
RMS_EPS = 1e-6

PEER_HEADS = 8
PEER_KEYS = 128
PEER_TOPK = 16
PEER_HALF_DIM = 128
PEER_EXPERTS = PEER_KEYS * PEER_KEYS

VMEM_LIMIT_BIG = 60 * 1024 * 1024
VMEM_LIMIT_MID = 48 * 1024 * 1024

NT_DIMS = (((1,), (1,)), ((), ()))


def _gelu_tanh(x):
    c = math.sqrt(2.0 / math.pi)
    return 0.5 * x * (1.0 + jnp.tanh(c * (x + 0.044715 * (x * x * x))))


def _rms_matmul_kernel(x_ref, g_ref, w_ref, o_ref, *rest, emit_ht):
    if emit_ht:
        ht_ref, h_scr = rest
    else:
        (h_scr,) = rest

    @pl.when(pl.program_id(1) == 0)
    def _():
        x = x_ref[...]
        ms = jnp.mean(x * x, axis=-1, keepdims=True)
        h = x * lax.rsqrt(ms + RMS_EPS) * g_ref[...]
        h_scr[...] = h.astype(BF16)
        if emit_ht:
            ht_ref[...] = h.T.astype(BF16)

    o_ref[...] = jnp.dot(h_scr[...], w_ref[...], preferred_element_type=F32)


def rms_matmul(x, g, w, *, tm, tn, emit_ht=False):
    n, d = x.shape
    m = w.shape[1]
    assert n % tm == 0 and m % tn == 0
    out_shape = [jax.ShapeDtypeStruct((n, m), F32)]
    out_specs = [pl.BlockSpec((tm, tn), lambda i, j: (i, j))]
    if emit_ht:
        out_shape.append(jax.ShapeDtypeStruct((d, n), BF16))
        out_specs.append(pl.BlockSpec((d, tm), lambda i, j: (0, i)))
    res = pl.pallas_call(
        functools.partial(_rms_matmul_kernel, emit_ht=emit_ht),
        grid=(n // tm, m // tn),
        in_specs=[
            pl.BlockSpec((tm, d), lambda i, j: (i, 0)),
            pl.BlockSpec((1, d), lambda i, j: (0, 0)),
            pl.BlockSpec((d, tn), lambda i, j: (0, j)),
        ],
        out_specs=out_specs,
        out_shape=out_shape,
        scratch_shapes=[pltpu.VMEM((tm, d), BF16)],
        compiler_params=pltpu.CompilerParams(
            dimension_semantics=("parallel", "arbitrary"), vmem_limit_bytes=VMEM_LIMIT_MID),
        name="rms_matmul_ht" if emit_ht else "rms_matmul",
    )(x, g.reshape(1, d), w)
    return res if emit_ht else res[0]


NEG_BIG = -jnp.inf
NOT_RANKED = 99.0


def _top16_rows(s, break_ties):
    rows = lax.broadcasted_iota(jnp.int32, s.shape, 0).astype(F32)
    rank = jnp.full(s.shape, NOT_RANKED, F32)
    vals = []
    for k in range(PEER_TOPK):
        m = jnp.max(s, axis=0, keepdims=True)
        if break_ties:
            idx = jnp.min(jnp.where(s == m, rows, float(PEER_KEYS)), axis=0, keepdims=True)
            hit = rows == idx
        else:
            hit = s == m
        rank = jnp.where(hit, float(k), rank)
        s = jnp.where(hit, NEG_BIG, s)
        vals.append(m)
    return vals, rank


def _pair_counts(a, b, break_ties):
    t = a[0].shape[1]
    b16 = jnp.concatenate(b, axis=0)
    a_hi = jnp.concatenate(a[8:], axis=0)
    sub8 = lax.broadcasted_iota(jnp.int32, (8, t), 0).astype(F32)
    sub16 = lax.broadcasted_iota(jnp.int32, (16, t), 0).astype(F32)
    cands = [a[0] + b16]
    flats = [sub16]
    for k0 in range(1, 8):
        lim = PEER_TOPK // (k0 + 1)
        c = a[k0] + b16[:8]
        if lim < 8:
            c = jnp.where(sub8 < float(lim), c, NEG_BIG)
        cands.append(c)
        flats.append(sub8 + float(k0 * 16))
    cands.append(a_hi + b[0])
    flats.append((sub8 + 8.0) * 16.0)

    cmax = a[0] + b[0]
    counts = [jnp.zeros(c.shape, F32) for c in cands]
    z = jnp.zeros((1, t), F32)
    for _ in range(PEER_TOPK):
        m = cands[0].max(axis=0, keepdims=True)
        for c in cands[1:]:
            m = jnp.maximum(m, c.max(axis=0, keepdims=True))
        if break_ties:
            idx = None
            for c, f in zip(cands, flats):
                i = jnp.min(jnp.where(c == m, f, 999.0), axis=0, keepdims=True)
                idx = i if idx is None else jnp.minimum(idx, i)
            hits = [f == idx for f in flats]
        else:
            hits = [c == m for c in cands]
        for j, hit in enumerate(hits):
            counts[j] = counts[j] + hit.astype(F32)
            cands[j] = jnp.where(hit, NEG_BIG, cands[j])
        z = z + jnp.exp(m - cmax)
    n = [c.sum(axis=0, keepdims=True) for c in counts[:8]]
    n += [counts[8][r:r + 1] for r in range(8)]
    return n, z


def _peer_route_kernel(q_ref, keys_ref, r1_ref, cnt_ref, a_ref, b_ref):
    def head(h, carry):
        col = pl.multiple_of(h * (2 * PEER_HALF_DIM), 2 * PEER_HALF_DIM)
        q0 = q_ref[:, pl.ds(col, PEER_HALF_DIM)].astype(BF16)
        q1 = q_ref[:, pl.ds(col + PEER_HALF_DIM, PEER_HALF_DIM)].astype(BF16)
        s0 = lax.dot_general(keys_ref[2 * h], q0, NT_DIMS, preferred_element_type=F32)
        s1 = lax.dot_general(keys_ref[2 * h + 1], q1, NT_DIMS, preferred_element_type=F32)

        def route(break_ties):
            v0, rank0 = _top16_rows(s0, break_ties)
            v1, rank1 = _top16_rows(s1, break_ties)
            n, z = _pair_counts(v0, v1, break_ties)
            cnt = jnp.zeros(s0.shape, F32)
            for k0 in range(PEER_TOPK):
                cnt = jnp.where(rank0 == float(k0), n[k0], cnt)
            r1_ref[h] = rank1.astype(r1_ref.dtype)
            cnt_ref[h] = cnt
            a_ref[h] = jnp.exp(s0 - v0[0])
            b_ref[h] = (jnp.exp(s1 - v1[0]) / z).astype(b_ref.dtype)
            ranked = (jnp.sum(jnp.where(rank0 < NOT_RANKED, 1.0, 0.0), axis=0, keepdims=True)
                      + jnp.sum(jnp.where(rank1 < NOT_RANKED, 1.0, 0.0), axis=0, keepdims=True))
            pairs = n[0]
            for nk in n[1:]:
                pairs = pairs + nk
            clean = jnp.logical_and(ranked == float(2 * PEER_TOPK), pairs == float(PEER_TOPK))
            return jnp.min(jnp.where(clean, 1.0, 0.0))

        all_clean = route(break_ties=False)

        @pl.when(all_clean < 0.5)
        def _():
            route(break_ties=True)

        return carry

    lax.fori_loop(0, PEER_HEADS, head, 0)


def peer_route(q, keys, *, tm):
    n = q.shape[0]
    assert n % tm == 0
    sds = [jax.ShapeDtypeStruct((PEER_HEADS, PEER_KEYS, n), dt) for dt in (BF16, F32, F32, BF16)]
    ospec = pl.BlockSpec((PEER_HEADS, PEER_KEYS, tm), lambda i: (0, 0, i))
    return pl.pallas_call(
        _peer_route_kernel,
        grid=(n // tm,),
        in_specs=[
            pl.BlockSpec((tm, PEER_HEADS * 2 * PEER_HALF_DIM), lambda i: (i, 0)),
            pl.BlockSpec((2 * PEER_HEADS, PEER_KEYS, PEER_HALF_DIM), lambda i: (0, 0, 0)),
        ],
        out_specs=[ospec] * 4,
        out_shape=sds,
        compiler_params=pltpu.CompilerParams(
            dimension_semantics=("parallel",), vmem_limit_bytes=VMEM_LIMIT_MID),
        name="peer_route",
    )(q, keys)


PEER_TE = 1024
PEER_ROWS = PEER_TE // PEER_KEYS


PEER_CHUNK = 16


def _peer_gate_block(act_ref, p_ref, r1_ref, b_ref, cnt_ref, a_ref, r, tc):
    cols = slice(tc * 128, (tc + 1) * 128)
    n_chunks = PEER_KEYS // PEER_CHUNK
    w = [jnp.zeros((PEER_CHUNK, 128), BF16) for _ in range(n_chunks)]
    zero = jnp.zeros((), BF16)
    for h in range(PEER_HEADS):
        cnt = jnp.broadcast_to(cnt_ref[h, 0, r:r + 1, cols], (PEER_CHUNK, 128)).astype(BF16)
        a = jnp.broadcast_to(a_ref[h, 0, r:r + 1, cols], (PEER_CHUNK, 128)).astype(BF16)
        for c in range(n_chunks):
            rows = slice(c * PEER_CHUNK, (c + 1) * PEER_CHUNK)
            first = jnp.minimum(jnp.maximum(cnt - r1_ref[h, rows, cols], zero), a)
            w[c] = w[c] + first * b_ref[h, rows, cols]
    for c in range(n_chunks):
        rows = slice(r * PEER_KEYS + c * PEER_CHUNK, r * PEER_KEYS + (c + 1) * PEER_CHUNK)
        g = _gelu_tanh(act_ref[rows, cols])
        p_ref[rows, cols] = (w[c].astype(F32) * g).astype(BF16)


def _peer_main_kernel(ht_ref, r1_ref, b_ref, cnt_ref, a_ref, u_ref, vt_ref, x_ref, o_ref,
                      acc_ref, act0_ref, act1_ref, p0_ref, p1_ref, *, tm, n_e):
    s = pl.program_id(1)
    acts = (act0_ref, act1_ref)
    ps = (p0_ref, p1_ref)

    def act_piece(parity, mh, nh):
        rows = slice(mh * (PEER_TE // 2), (mh + 1) * (PEER_TE // 2))
        cols = slice(nh * (tm // 2), (nh + 1) * (tm // 2))
        acts[parity][rows, cols] = jnp.dot(u_ref[rows, :], ht_ref[:, cols], preferred_element_type=F32)

    def out_piece(parity, mh, nh):
        d = acc_ref.shape[0]
        rows = slice(mh * (d // 2), (mh + 1) * (d // 2))
        cols = slice(nh * (tm // 2), (nh + 1) * (tm // 2))
        acc_ref[rows, cols] += jnp.dot(vt_ref[rows, :], ps[parity][:, cols], preferred_element_type=F32)

    def step(parity, do_act, do_gates, do_out):
        halves = [(mh, nh) for mh in range(2) for nh in range(2)]
        pieces = [functools.partial(act_piece, parity, *h) for h in halves] if do_act else []
        pieces += [functools.partial(out_piece, parity, *h) for h in halves] if do_out else []
        blocks = [(r, tc) for r in range(PEER_ROWS) for tc in range(tm // 128)] if do_gates else []
        per = -(-len(blocks) // max(len(pieces), 1))
        for i, piece in enumerate(pieces):
            piece()
            for r, tc in blocks[i * per:(i + 1) * per]:
                _peer_gate_block(acts[1 - parity], ps[1 - parity], r1_ref, b_ref, cnt_ref, a_ref, r, tc)
        for r, tc in blocks[len(pieces) * per:]:
            _peer_gate_block(acts[1 - parity], ps[1 - parity], r1_ref, b_ref, cnt_ref, a_ref, r, tc)

    @pl.when(s == 0)
    def _():
        acc_ref[...] = jnp.zeros_like(acc_ref)
        step(0, True, False, False)

    @pl.when(s == 1)
    def _():
        step(1, True, True, False)

    for parity in range(2):
        @pl.when(jnp.logical_and(jnp.logical_and(s >= 2, s < n_e), s % 2 == parity))
        def _():
            step(parity, True, True, True)

    @pl.when(s == n_e)
    def _():
        step(n_e % 2, False, True, True)

    @pl.when(s == n_e + 1)
    def _():
        step((n_e + 1) % 2, False, False, True)
        o_ref[...] = x_ref[...] + acc_ref[...].T


def peer_main(ht, route, u_all, vt_all, layer, x, *, tm):
    d, n = ht.shape
    rank1, cnt0, a, b = route
    n_e = PEER_EXPERTS // PEER_TE
    rows_shape = (PEER_HEADS, n_e, PEER_ROWS, n)
    clamp = lambda v: jnp.clip(v, 0, n_e - 1)
    full_spec = pl.BlockSpec((PEER_HEADS, PEER_KEYS, tm), lambda i, s: (0, 0, i))
    rows_spec = pl.BlockSpec((PEER_HEADS, 1, PEER_ROWS, tm), lambda i, s: (0, clamp(s - 1), 0, i))
    return pl.pallas_call(
        functools.partial(_peer_main_kernel, tm=tm, n_e=n_e),
        grid=(n // tm, n_e + 2),
        in_specs=[
            pl.BlockSpec((d, tm), lambda i, s: (0, i)),
            full_spec, full_spec, rows_spec, rows_spec,
            pl.BlockSpec((None, PEER_TE, d), lambda i, s: (layer, clamp(s), 0)),
            pl.BlockSpec((None, d, PEER_TE), lambda i, s: (layer, 0, clamp(s - 2))),
            pl.BlockSpec((tm, d), lambda i, s: (i, 0)),
        ],
        out_specs=pl.BlockSpec((tm, d), lambda i, s: (i, 0)),
        out_shape=jax.ShapeDtypeStruct((n, d), F32),
        scratch_shapes=[
            pltpu.VMEM((d, tm), F32),
            pltpu.VMEM((PEER_TE, tm), F32),
            pltpu.VMEM((PEER_TE, tm), F32),
            pltpu.VMEM((PEER_TE, tm), BF16),
            pltpu.VMEM((PEER_TE, tm), BF16),
        ],
        compiler_params=pltpu.CompilerParams(
            dimension_semantics=("parallel", "arbitrary"), vmem_limit_bytes=VMEM_LIMIT_BIG),
        name="peer_main",
    )(ht, rank1, b, cnt0.reshape(rows_shape), a.reshape(rows_shape), u_all, vt_all, x)


def peer_layer(x, g, w_q, keys, u_all, vt_all, layer, *, tm_q=512, tm_route=256, tm_main=512):
    q, ht = rms_matmul(x, g, w_q, tm=tm_q, tn=1024, emit_ht=True)
    route = peer_route(q, keys, tm=tm_route)
    return peer_main(ht, route, u_all, vt_all, layer, x, tm=tm_main)


CONV_TM = 256
CONV_TAIL = 8


def _conv_core_kernel(b_ref, c_ref, h_ref, x_ref, s0_ref, s1_ref, cw_ref, wout_ref,
                      o_ref, tail_ref, us_ref, ubuf, *, prompt_tiles, tiles_per_seq):
    i = pl.program_id(0)
    tm = CONV_TM
    u = c_ref[...] * h_ref[...]
    w0 = cw_ref[0:1, :]
    w1 = cw_ref[1:2, :]
    w2 = cw_ref[2:3, :]

    @pl.when(i % tiles_per_seq == 0)
    def _():
        ubuf[0:CONV_TAIL, :] = jnp.zeros((CONV_TAIL, D_MODEL), F32)

    ubuf[CONV_TAIL:CONV_TAIL + tm, :] = u
    tail_ref[0] = u[tm - CONV_TAIL:, :]

    @pl.when(i < prompt_tiles)
    def _():
        z = ubuf[CONV_TAIL - 2:CONV_TAIL - 2 + tm, :] * w0 + ubuf[CONV_TAIL - 1:CONV_TAIL - 1 + tm, :] * w1 + u * w2
        y = jnp.dot((b_ref[...] * z).astype(BF16), wout_ref[...], preferred_element_type=F32)
        o_ref[...] = x_ref[...] + y
        ubuf[0:CONV_TAIL, :] = u[tm - CONV_TAIL:, :]

    @pl.when(i >= prompt_tiles)
    def _():
        z = s0_ref[...] * w0 + s1_ref[...] * w1 + u * w2
        y = jnp.dot((b_ref[...] * z).astype(BF16), wout_ref[...], preferred_element_type=F32)
        o_ref[...] = x_ref[...] + y
        us_ref[...] = u


def conv_core(p, x, s0, s1, cw, wout, *, n_prompt, seq_len):
    n, d = x.shape
    tm = CONV_TM
    n_tiles = n // tm
    prompt_tiles = n_prompt // tm
    tiles_per_seq = seq_len // tm
    n_dec = n - n_prompt
    dec_map = lambda i: (jnp.maximum(i - prompt_tiles, 0), 0)
    return pl.pallas_call(
        functools.partial(_conv_core_kernel, prompt_tiles=prompt_tiles, tiles_per_seq=tiles_per_seq),
        grid=(n_tiles,),
        in_specs=[
            pl.BlockSpec((tm, d), lambda i: (i, 0)),
            pl.BlockSpec((tm, d), lambda i: (i, 1)),
            pl.BlockSpec((tm, d), lambda i: (i, 2)),
            pl.BlockSpec((tm, d), lambda i: (i, 0)),
            pl.BlockSpec((tm, d), dec_map),
            pl.BlockSpec((tm, d), dec_map),
            pl.BlockSpec((3, d), lambda i: (0, 0)),
            pl.BlockSpec((d, d), lambda i: (0, 0)),
        ],
        out_specs=[
            pl.BlockSpec((tm, d), lambda i: (i, 0)),
            pl.BlockSpec((1, CONV_TAIL, d), lambda i: (i, 0, 0)),
            pl.BlockSpec((tm, d), dec_map),
        ],
        out_shape=[
            jax.ShapeDtypeStruct((n, d), F32),
            jax.ShapeDtypeStruct((n_tiles, CONV_TAIL, d), F32),
            jax.ShapeDtypeStruct((n_dec, d), F32),
        ],
        scratch_shapes=[pltpu.VMEM((CONV_TAIL + tm, d), F32)],
        compiler_params=pltpu.CompilerParams(
            dimension_semantics=("arbitrary",), vmem_limit_bytes=VMEM_LIMIT_BIG),
        name="conv_core",
    )(p, p, p, x, s0, s1, cw, wout)


N_HEADS = 16
N_KV_HEADS = 4
HPG = N_HEADS // N_KV_HEADS
HEAD_DIM = 128
Q_DIM = N_HEADS * HEAD_DIM
KV_DIM = N_KV_HEADS * HEAD_DIM
CMP_BLOCK = 64
N_SELECT = 16
WINDOW = 512
ROPE_THETA = 10000.0
NEG_INF = -1e30
FORCED_SCORE = 1e6


def _rope(x, cos, sin_signed):
    return x * cos + pltpu.roll(x, HEAD_DIM // 2, 1) * sin_signed


def _nsa_layout_kernel(p_ref, cos_ref, sin_ref, q_ref, rows_ref, win_ref):
    cos = cos_ref[...]
    sin = sin_ref[...]
    scale = HEAD_DIM ** -0.5
    for h in range(N_HEADS):
        q_ref[0, h] = (_rope(p_ref[:, h * HEAD_DIM:(h + 1) * HEAD_DIM], cos, sin) * scale).astype(BF16)
    for c in range(6):
        for g in range(N_KV_HEADS):
            col = Q_DIM + c * KV_DIM + g * HEAD_DIM
            v = p_ref[:, col:col + HEAD_DIM]
            if c % 2 == 0:
                v = _rope(v, cos, sin)
            if c < 4:
                rows_ref[0, 0, c, g] = v
            else:
                win_ref[0, c - 4, g] = v


def nsa_layout(p, cos, sin, *, n_seq, seq_len, row0, tq):
    width = Q_DIM + 6 * KV_DIM
    n_t = seq_len // tq
    blk0 = row0 // tq
    assert row0 % tq == 0
    return pl.pallas_call(
        _nsa_layout_kernel,
        grid=(n_seq, n_t),
        in_specs=[
            pl.BlockSpec((tq, width), lambda b, t: (blk0 + b * n_t + t, 0)),
            pl.BlockSpec((tq, HEAD_DIM), lambda b, t: (t, 0)),
            pl.BlockSpec((tq, HEAD_DIM), lambda b, t: (t, 0)),
        ],
        out_specs=[
            pl.BlockSpec((1, N_HEADS, tq, HEAD_DIM), lambda b, t: (b, 0, t, 0)),
            pl.BlockSpec((1, 1, 4, N_KV_HEADS, tq, HEAD_DIM), lambda b, t: (b, 0, 0, 0, t, 0)),
            pl.BlockSpec((1, 2, N_KV_HEADS, tq, HEAD_DIM), lambda b, t: (b, 0, 0, t, 0)),
        ],
        out_shape=[
            jax.ShapeDtypeStruct((n_seq, N_HEADS, seq_len, HEAD_DIM), BF16),
            jax.ShapeDtypeStruct((n_seq, 1, 4, N_KV_HEADS, seq_len, HEAD_DIM), F32),
            jax.ShapeDtypeStruct((n_seq, 2, N_KV_HEADS, seq_len, HEAD_DIM), F32),
        ],
        compiler_params=pltpu.CompilerParams(
            dimension_semantics=("parallel", "parallel"), vmem_limit_bytes=VMEM_LIMIT_MID),
        name="nsa_layout",
    )(p, cos, sin)


def rope_tables(pos):
    half = HEAD_DIM // 2
    inv = ROPE_THETA ** (-jnp.arange(half, dtype=F32) / half)
    ang = pos.astype(F32)[:, None] * inv[None, :]
    cos = jnp.cos(ang)
    sin = jnp.sin(ang)
    return jnp.concatenate([cos, cos], axis=1), jnp.concatenate([-sin, sin], axis=1)


CMP_HIDDEN = 256
CMP_PAIR = 2
CMP_PASSES = CMP_BLOCK // CMP_PAIR


def _compress_rows(position_rows, n_rows, pe_ref, w1_ref, w2_ref):
    acc = jnp.zeros((n_rows, CMP_HIDDEN), F32)
    for p in range(CMP_PASSES):
        pair = [position_rows(CMP_PAIR * p + j) + pe_ref[CMP_PAIR * p + j:CMP_PAIR * p + j + 1, :]
                for j in range(CMP_PAIR)]
        kb = jnp.concatenate(pair, axis=1).astype(BF16)
        acc = acc + jnp.dot(kb, w1_ref[p], preferred_element_type=F32)
    hid = _gelu_tanh(acc)
    return jnp.dot(hid.astype(BF16), w2_ref[...], preferred_element_type=F32)


def _compress_kernel(blk_ref, x_ref, pe_ref, w1_ref, w2_ref, o_ref):
    del blk_ref
    o_ref[...] = _compress_rows(lambda l: x_ref[:, l, :], x_ref.shape[0], pe_ref, w1_ref, w2_ref)


def _compress_weight_specs():
    const = lambda n: (lambda *args: (0,) * n)
    return [
        pl.BlockSpec((CMP_BLOCK, HEAD_DIM), const(2)),
        pl.BlockSpec((CMP_PASSES, CMP_PAIR * HEAD_DIM, CMP_HIDDEN), const(3)),
        pl.BlockSpec((CMP_HIDDEN, HEAD_DIM), const(2)),
    ]


def compress_blocks(x, pe, w1, w2, *, tr, blocks):
    blocks = jnp.asarray(blocks, jnp.int32)
    nb = blocks.shape[0]
    return pl.pallas_call(
        _compress_kernel,
        grid_spec=pltpu.PrefetchScalarGridSpec(
            num_scalar_prefetch=1,
            grid=(nb,),
            in_specs=[pl.BlockSpec((tr, CMP_BLOCK, HEAD_DIM), lambda i, blk: (blk[i], 0, 0))]
            + _compress_weight_specs(),
            out_specs=pl.BlockSpec((tr, HEAD_DIM), lambda i, blk: (i, 0)),
        ),
        out_shape=jax.ShapeDtypeStruct((nb * tr, HEAD_DIM), F32),
        compiler_params=pltpu.CompilerParams(
            dimension_semantics=("arbitrary",), vmem_limit_bytes=VMEM_LIMIT_MID),
        name="compress_blocks",
    )(blocks, x, pe, w1, w2)


ATT_TQ = 256
ATT_TK = 512


def _softmax_masked(s, mask, axis):
    sm = jnp.where(mask, s, NEG_INF)
    e = jnp.exp(sm - jnp.max(sm, axis=axis, keepdims=True))
    p = e / jnp.sum(e, axis=axis, keepdims=True)
    return jnp.where(mask, p, 0.0)


def _block_ranks(score_t, n_real):
    blk = lax.broadcasted_iota(jnp.int32, score_t.shape, 0)
    rank = jnp.zeros(score_t.shape, F32)
    for m in range(n_real):
        row = score_t[m:m + 1, :]
        earlier = jnp.where(blk > m, 1.0, 0.0)
        rank = rank + jnp.where(row > score_t, 1.0, jnp.where(row == score_t, earlier, 0.0))
    return rank


def _select_blocks(score_t):
    return jnp.where(_block_ranks(score_t, score_t.shape[0]) < float(N_SELECT), 1.0, 0.0)


def _flash(q_ref, k_tile, v_tile, kt_lo, kt_hi, bias_of, m_scr, l_scr, acc_scr):
    m_scr[...] = jnp.full(m_scr.shape, NEG_INF, F32)
    l_scr[...] = jnp.zeros(l_scr.shape, F32)
    acc_scr[...] = jnp.zeros(acc_scr.shape, F32)
    q2 = q_ref[0].reshape(HPG * ATT_TQ, HEAD_DIM)

    def body(kt, carry):
        s = lax.dot_general(q2, k_tile(kt), NT_DIMS, preferred_element_type=F32).reshape(HPG, ATT_TQ, ATT_TK)
        s = s + bias_of(kt)[None]
        m_old = m_scr[...]
        m_new = jnp.maximum(m_old, jnp.max(s, axis=-1, keepdims=True))
        alpha = jnp.exp(m_old - m_new)
        p = jnp.exp(s - m_new)
        l_scr[...] = alpha * l_scr[...] + jnp.sum(p, axis=-1, keepdims=True)
        pv = jnp.dot(p.reshape(HPG * ATT_TQ, ATT_TK).astype(BF16), v_tile(kt), preferred_element_type=F32)
        acc_scr[...] = alpha * acc_scr[...] + pv.reshape(HPG, ATT_TQ, HEAD_DIM)
        m_scr[...] = m_new
        return carry

    lax.fori_loop(kt_lo, kt_hi, body, 0)
    return acc_scr[...] / l_scr[...]


def _nsa_prompt_attn_kernel(q_ref, kc_ref, vc_ref, ks_ref, vs_ref, kw_ref, vw_ref, gate_ref, o_ref,
                            m_scr, l_scr, acc_scr):
    tq = ATT_TQ
    qi = pl.program_id(2)
    q0 = qi * tq
    q3 = q_ref[0]
    q2 = q3.reshape(HPG * tq, HEAD_DIM)
    nb = kc_ref.shape[2]
    kc = kc_ref[0, 0].astype(BF16)
    vc = vc_ref[0, 0].astype(BF16)

    pos_col = q0 + lax.broadcasted_iota(jnp.int32, (tq, 1), 0)
    blk_row = lax.broadcasted_iota(jnp.int32, (1, nb), 1)
    c_mask = ((blk_row + 1) * CMP_BLOCK - 1) <= pos_col
    s_c = lax.dot_general(q2, kc, NT_DIMS, preferred_element_type=F32).reshape(HPG, tq, nb)
    p_c = _softmax_masked(s_c, c_mask[None], axis=-1)
    o_c = jnp.dot(p_c.reshape(HPG * tq, nb).astype(BF16), vc, preferred_element_type=F32).reshape(HPG, tq, HEAD_DIM)

    pos_row = q0 + lax.broadcasted_iota(jnp.int32, (1, tq), 1)
    blk_col = lax.broadcasted_iota(jnp.int32, (nb, 1), 0)
    c_mask_t = ((blk_col + 1) * CMP_BLOCK - 1) <= pos_row
    imp = jnp.zeros((nb, tq), F32)
    for h in range(HPG):
        s_t = lax.dot_general(kc, q3[h], NT_DIMS, preferred_element_type=F32)
        imp = imp + _softmax_masked(s_t, c_mask_t, axis=0)
    cur = pos_row // CMP_BLOCK
    reach = blk_col <= cur
    forced = jnp.logical_and(reach, jnp.logical_or(blk_col == 0, blk_col >= cur - 1))
    score = jnp.where(forced, FORCED_SCORE, jnp.where(reach, imp, NEG_INF))
    sel_t = _select_blocks(score).astype(BF16)
    eye = jnp.where(lax.broadcasted_iota(jnp.int32, (tq, tq), 0) == lax.broadcasted_iota(jnp.int32, (tq, tq), 1),
                    1.0, 0.0).astype(BF16)
    sel = lax.dot_general(eye, sel_t, NT_DIMS, preferred_element_type=F32).astype(BF16)

    def key_pos(kt):
        return kt * ATT_TK + lax.broadcasted_iota(jnp.int32, (1, ATT_TK), 1)

    def tile_of(ref, lead):
        def get(kt):
            return ref[lead + (pl.ds(pl.multiple_of(kt * ATT_TK, ATT_TK), ATT_TK), slice(None))].astype(BF16)
        return get

    def sel_bias(kt):
        kpos = key_pos(kt)
        expand = jnp.where((kpos // CMP_BLOCK) == blk_col, 1.0, 0.0).astype(BF16)
        in_sel = jnp.dot(sel, expand, preferred_element_type=F32) > 0.5
        return jnp.where(jnp.logical_and(in_sel, kpos <= pos_col), 0.0, NEG_INF)

    scratch = (m_scr, l_scr, acc_scr)
    kt_hi = (q0 + tq + ATT_TK - 1) // ATT_TK
    lead_s = (0, 0, 0, 0)
    o_s = _flash(q_ref, tile_of(ks_ref, lead_s), tile_of(vs_ref, lead_s), 0, kt_hi, sel_bias, *scratch)
    n_w = WINDOW + tq
    w0 = pl.multiple_of(jnp.maximum(q0 - WINDOW, 0), tq)
    dist = pos_col - (w0 + lax.broadcasted_iota(jnp.int32, (1, n_w), 1))
    w_bias = jnp.where(jnp.logical_and(dist >= 0, dist <= WINDOW), 0.0, NEG_INF)
    kw = kw_ref[0, 0, 0, pl.ds(w0, n_w), :].astype(BF16)
    vw = vw_ref[0, 0, 0, pl.ds(w0, n_w), :].astype(BF16)
    s_w = lax.dot_general(q2, kw, NT_DIMS, preferred_element_type=F32).reshape(HPG, tq, n_w) + w_bias[None]
    p_w = jnp.exp(s_w - jnp.max(s_w, axis=-1, keepdims=True))
    l_w = jnp.sum(p_w, axis=-1, keepdims=True)
    o_w = jnp.dot(p_w.reshape(HPG * tq, n_w).astype(BF16), vw, preferred_element_type=F32)
    o_w = o_w.reshape(HPG, tq, HEAD_DIM) / l_w

    gate = jax.nn.sigmoid(gate_ref[0, 0])
    for h in range(HPG):
        o = (gate[:, 3 * h:3 * h + 1] * o_c[h] + gate[:, 3 * h + 1:3 * h + 2] * o_s[h]
             + gate[:, 3 * h + 2:3 * h + 3] * o_w[h])
        o_ref[:, h * HEAD_DIM:(h + 1) * HEAD_DIM] = o


def nsa_prompt_attn(q, kc, vc, rows, win, gates):
    b, _, t, _ = q.shape
    assert t % ATT_TK == 0 and t >= WINDOW + ATT_TQ
    nb = kc.shape[2]
    n_q = t // ATT_TQ
    kv_spec = lambda comp: pl.BlockSpec((1, 1, 1, 1, t, HEAD_DIM), lambda bi, g, qi: (bi, 0, comp, g, 0, 0))
    win_spec = lambda comp: pl.BlockSpec((1, 1, 1, t, HEAD_DIM), lambda bi, g, qi: (bi, comp, g, 0, 0))
    cmp_spec = pl.BlockSpec((1, 1, nb, HEAD_DIM), lambda bi, g, qi: (bi, g, 0, 0))
    return pl.pallas_call(
        _nsa_prompt_attn_kernel,
        grid=(b, N_KV_HEADS, n_q),
        in_specs=[
            pl.BlockSpec((1, HPG, ATT_TQ, HEAD_DIM), lambda bi, g, qi: (bi, g, qi, 0)),
            cmp_spec, cmp_spec,
            kv_spec(2), kv_spec(3),
            win_spec(0), win_spec(1),
            pl.BlockSpec((1, 1, ATT_TQ, 3 * HPG), lambda bi, g, qi: (bi, g, qi, 0)),
        ],
        out_specs=pl.BlockSpec((ATT_TQ, HPG * HEAD_DIM), lambda bi, g, qi: (bi * n_q + qi, g)),
        out_shape=jax.ShapeDtypeStruct((b * t, Q_DIM), F32),
        scratch_shapes=[
            pltpu.VMEM((HPG, ATT_TQ, 1), F32),
            pltpu.VMEM((HPG, ATT_TQ, 1), F32),
            pltpu.VMEM((HPG, ATT_TQ, HEAD_DIM), F32),
        ],
        compiler_params=pltpu.CompilerParams(
            dimension_semantics=("parallel", "parallel", "arbitrary"), vmem_limit_bytes=VMEM_LIMIT_MID),
        name="nsa_prompt_attn",
    )(q, kc, vc, rows, rows, win, win, gates)


PAGE_SIZE = 128
DEC_ROWS = 512
DEC_NB_PAD = 128
NSA_PROJ_PAD = 5632
BLOCKS_PER_PAGE = PAGE_SIZE // CMP_BLOCK
PAGE_ROWS = N_KV_HEADS * BLOCKS_PER_PAGE
CMP_PAGES_PER_STEP = 32


def _paged_compress_kernel(pt_ref, cache_ref, pe_ref, w1_ref, w2_ref, o_ref, buf, sem, *, comp):
    s = pl.program_id(0)
    n_steps = pl.num_programs(0)

    def block_copy(step, p, row):
        slot = step % 2
        page = pt_ref[step * CMP_PAGES_PER_STEP + p]
        return pltpu.make_async_copy(cache_ref.at[page, comp, row], buf.at[slot, :, p * PAGE_ROWS + row, :],
                                     sem.at[slot])

    def for_all_blocks(step, fn):
        for p in range(CMP_PAGES_PER_STEP):
            for row in range(PAGE_ROWS):
                fn(block_copy(step, p, row))

    @pl.when(s == 0)
    def _():
        for_all_blocks(0, lambda cp: cp.start())

    @pl.when(s + 1 < n_steps)
    def _():
        for_all_blocks(s + 1, lambda cp: cp.start())

    for_all_blocks(s, lambda cp: cp.wait())
    slot = s % 2
    n_rows = CMP_PAGES_PER_STEP * PAGE_ROWS
    o_ref[...] = _compress_rows(lambda l: buf[slot, l], n_rows, pe_ref, w1_ref, w2_ref)


def paged_compress(page_table_flat, cache, pe, w1, w2, *, comp):
    n_pages = page_table_flat.shape[0]
    assert n_pages % CMP_PAGES_PER_STEP == 0
    rows = CMP_PAGES_PER_STEP * PAGE_ROWS
    return pl.pallas_call(
        functools.partial(_paged_compress_kernel, comp=comp),
        grid_spec=pltpu.PrefetchScalarGridSpec(
            num_scalar_prefetch=1,
            grid=(n_pages // CMP_PAGES_PER_STEP,),
            in_specs=[pl.BlockSpec(memory_space=pl.ANY)] + _compress_weight_specs(),
            out_specs=pl.BlockSpec((rows, HEAD_DIM), lambda i, pt: (i, 0)),
            scratch_shapes=[
                pltpu.VMEM((2, CMP_BLOCK, rows, HEAD_DIM), F32),
                pltpu.SemaphoreType.DMA((2,)),
            ],
        ),
        out_shape=jax.ShapeDtypeStruct((n_pages * PAGE_ROWS, HEAD_DIM), F32),
        compiler_params=pltpu.CompilerParams(
            dimension_semantics=("arbitrary",), vmem_limit_bytes=VMEM_LIMIT_MID),
        name="paged_compress",
    )(page_table_flat, cache, pe, w1, w2)


def _group_diag(rows_per_group, cols_per_group, shape):
    r = lax.broadcasted_iota(jnp.int32, shape, 0) // rows_per_group
    c = lax.broadcasted_iota(jnp.int32, shape, 1) // cols_per_group
    return r == c


def _decode_attn_a_kernel(q_ref, kc_ref, vc_ref, new_ref, gate_ref, win_ref,
                          ocw_ref, idx_ref, wout_ref, *, nb, nb_pad, pos):
    q = q_ref[0]
    kc = kc_ref[0].reshape(N_KV_HEADS * nb_pad, HEAD_DIM).astype(BF16)
    vc = vc_ref[0].reshape(N_KV_HEADS * nb_pad, HEAD_DIM).astype(BF16)
    gate = jax.nn.sigmoid(gate_ref[0])

    shape = (N_HEADS, N_KV_HEADS * nb_pad)
    blk = lax.broadcasted_iota(jnp.int32, shape, 1) % nb_pad
    complete = jnp.logical_and((blk + 1) * CMP_BLOCK - 1 <= pos, blk < nb)
    c_mask = jnp.logical_and(_group_diag(HPG, nb_pad, shape), complete)
    s_c = lax.dot_general(q, kc, NT_DIMS, preferred_element_type=F32)
    p_c = _softmax_masked(s_c, c_mask, axis=-1)
    o_c = jnp.dot(p_c.astype(BF16), vc, preferred_element_type=F32)

    blk_col = lax.broadcasted_iota(jnp.int32, (nb_pad, 1), 0)
    complete_t = jnp.logical_and((blk_col + 1) * CMP_BLOCK - 1 <= pos, blk_col < nb)
    lane16 = lax.broadcasted_iota(jnp.int32, (nb_pad, N_HEADS), 1)
    lane = lax.broadcasted_iota(jnp.int32, (nb_pad, 128), 1)
    imp = jnp.zeros((nb_pad, 128), F32)
    for g in range(N_KV_HEADS):
        s_t = lax.dot_general(kc[g * nb_pad:(g + 1) * nb_pad], q, NT_DIMS, preferred_element_type=F32)
        p_t = _softmax_masked(s_t, complete_t, axis=0)
        imp_g = jnp.sum(jnp.where(lane16 // HPG == g, p_t, 0.0), axis=1, keepdims=True)
        imp = jnp.where(lane == g, imp_g, imp)
    cur = pos // CMP_BLOCK
    reach = jnp.logical_and(blk_col <= cur, blk_col < nb)
    forced = jnp.logical_and(reach, jnp.logical_or(blk_col == 0, blk_col >= cur - 1))
    score = jnp.where(forced, FORCED_SCORE, jnp.where(reach, imp, NEG_INF))
    score = jnp.where(blk_col < nb, score, 2.0 * NEG_INF)
    rank = _block_ranks(score, nb)
    blk_f = blk_col.astype(F32)
    picked = [jnp.sum(jnp.where(rank == float(k), blk_f, 0.0), axis=0, keepdims=True) for k in range(N_SELECT)]
    idx_ref[0] = jnp.concatenate(picked, axis=0).astype(jnp.int32)

    wb = win_ref.shape[4]
    kw = win_ref[0, 0, 0].reshape(N_KV_HEADS * wb, HEAD_DIM).astype(BF16)
    vw = win_ref[0, 0, 1].reshape(N_KV_HEADS * wb, HEAD_DIM).astype(BF16)
    k_new = new_ref[0, 4]
    v_new = new_ref[0, 5]
    shape_w = (N_HEADS, N_KV_HEADS * wb)
    w_mask = _group_diag(HPG, wb, shape_w)
    s_w = jnp.where(w_mask, lax.dot_general(q, kw, NT_DIMS, preferred_element_type=F32), NEG_INF)
    s_n = jnp.sum(q.astype(F32) * k_new, axis=-1, keepdims=True)
    m = jnp.maximum(jnp.max(s_w, axis=-1, keepdims=True), s_n)
    e_w = jnp.where(w_mask, jnp.exp(s_w - m), 0.0)
    e_n = jnp.exp(s_n - m)
    denom = jnp.sum(e_w, axis=-1, keepdims=True) + e_n
    o_w = (jnp.dot(e_w.astype(BF16), vw, preferred_element_type=F32)
           + e_n * v_new) / denom

    ocw_ref[0] = gate[:, 0:1] * o_c + gate[:, 2:3] * o_w

    for c in range(2):
        for g in range(N_KV_HEADS):
            wout_ref[0, 0, c, g, 0:wb - 1, :] = win_ref[0, 0, c, g, 1:wb, :]
            wout_ref[0, 0, c, g, wb - 1:wb, :] = new_ref[0, 4 + c, g * HPG:g * HPG + 1, :]


def decode_attn_a(q, kc, vc, new_rows, gates, win, *, nb, pos):
    s = q.shape[0]
    nb_pad = kc.shape[2]
    wb = win.shape[4]
    cmp_spec = pl.BlockSpec((1, N_KV_HEADS, nb_pad, HEAD_DIM), lambda i: (i, 0, 0, 0))
    win_spec = pl.BlockSpec((1, 1, 2, N_KV_HEADS, wb, HEAD_DIM), lambda i: (i, 0, 0, 0, 0, 0))
    return pl.pallas_call(
        functools.partial(_decode_attn_a_kernel, nb=nb, nb_pad=nb_pad, pos=pos),
        grid=(s,),
        in_specs=[
            pl.BlockSpec((1, N_HEADS, HEAD_DIM), lambda i: (i, 0, 0)),
            cmp_spec, cmp_spec,
            pl.BlockSpec((1, 6, N_HEADS, HEAD_DIM), lambda i: (i, 0, 0, 0)),
            pl.BlockSpec((1, N_HEADS, 3), lambda i: (i, 0, 0)),
            win_spec,
        ],
        out_specs=[
            pl.BlockSpec((1, N_HEADS, HEAD_DIM), lambda i: (i, 0, 0)),
            pl.BlockSpec((1, N_SELECT, 128), lambda i: (i, 0, 0)),
            win_spec,
        ],
        out_shape=[
            jax.ShapeDtypeStruct((s, N_HEADS, HEAD_DIM), F32),
            jax.ShapeDtypeStruct((s, N_SELECT, 128), jnp.int32),
            jax.ShapeDtypeStruct(win.shape, F32),
        ],
        compiler_params=pltpu.CompilerParams(
            dimension_semantics=("parallel",), vmem_limit_bytes=VMEM_LIMIT_MID),
        name="decode_attn_a",
    )(q, kc, vc, new_rows, gates, win)


SEL_SLOTS = N_KV_HEADS * N_SELECT


def _decode_attn_b_kernel(pt_ref, idx_ref, q_ref, cache_ref, new_ref, gate_ref, ocw_ref, o_ref,
                          kbuf, vbuf, sem, *, pages_per_seq, pos):
    s = pl.program_id(0)
    n_seq = pl.num_programs(0)
    nb_past = pages_per_seq * BLOCKS_PER_PAGE
    new_blk = pos // CMP_BLOCK
    blks = [idx_ref[s * SEL_SLOTS + slot] for slot in range(SEL_SLOTS)]

    def block_copies(seq, slot):
        buf = seq % 2
        blk = jnp.minimum(idx_ref[seq * SEL_SLOTS + slot], nb_past - 1)
        page = pt_ref[seq * pages_per_seq + blk // BLOCKS_PER_PAGE]
        row = (slot // N_SELECT) * BLOCKS_PER_PAGE + blk % BLOCKS_PER_PAGE
        return (pltpu.make_async_copy(cache_ref.at[page, 2, row], kbuf.at[buf, slot], sem.at[buf, 0]),
                pltpu.make_async_copy(cache_ref.at[page, 3, row], vbuf.at[buf, slot], sem.at[buf, 1]))

    def for_all_blocks(seq, fn):
        for slot in range(SEL_SLOTS):
            for cp in block_copies(seq, slot):
                fn(cp)

    @pl.when(s == 0)
    def _():
        for_all_blocks(0, lambda cp: cp.start())

    @pl.when(s + 1 < n_seq)
    def _():
        for_all_blocks(s + 1, lambda cp: cp.start())

    for_all_blocks(s, lambda cp: cp.wait())
    cur = s % 2

    q = q_ref[0]
    n_keys = SEL_SLOTS * CMP_BLOCK
    shape = (N_HEADS, n_keys)
    lane = lax.broadcasted_iota(jnp.int32, (8, SEL_SLOTS), 1)
    valid = jnp.zeros((8, SEL_SLOTS), F32)
    for slot in range(SEL_SLOTS):
        valid = jnp.where(lane == slot, jnp.where(blks[slot] < nb_past, 1.0, 0.0), valid)
    expand = jnp.where(lax.broadcasted_iota(jnp.int32, (SEL_SLOTS, n_keys), 1) // CMP_BLOCK
                       == lax.broadcasted_iota(jnp.int32, (SEL_SLOTS, n_keys), 0), 1.0, 0.0).astype(BF16)
    valid_keys = jnp.dot(valid.astype(BF16), expand, preferred_element_type=F32)[0:1, :] > 0.5
    ok = jnp.logical_and(_group_diag(HPG, N_SELECT * CMP_BLOCK, shape), valid_keys)

    head_group = lax.broadcasted_iota(jnp.int32, (N_HEADS, 1), 0) // HPG
    sel_new = jnp.zeros((N_HEADS, 1), F32)
    for g in range(N_KV_HEADS):
        hit = blks[g * N_SELECT] == new_blk
        for k in range(1, N_SELECT):
            hit = jnp.logical_or(hit, blks[g * N_SELECT + k] == new_blk)
        sel_new = jnp.where(head_group == g, jnp.where(hit, 1.0, 0.0), sel_new)
    sel_new = sel_new > 0.5

    k_sel = kbuf[cur].reshape(n_keys, HEAD_DIM).astype(BF16)
    v_sel = vbuf[cur].reshape(n_keys, HEAD_DIM).astype(BF16)
    k_new = new_ref[0, 2]
    v_new = new_ref[0, 3]
    s_sel = jnp.where(ok, lax.dot_general(q, k_sel, NT_DIMS, preferred_element_type=F32), NEG_INF)
    s_new = jnp.where(sel_new, jnp.sum(q.astype(F32) * k_new, axis=-1, keepdims=True), NEG_INF)
    m = jnp.maximum(jnp.max(s_sel, axis=-1, keepdims=True), s_new)
    e_sel = jnp.where(ok, jnp.exp(s_sel - m), 0.0)
    e_new = jnp.where(sel_new, jnp.exp(s_new - m), 0.0)
    l = jnp.sum(e_sel, axis=-1, keepdims=True) + e_new
    acc = jnp.dot(e_sel.astype(BF16), v_sel, preferred_element_type=F32) + e_new * v_new
    o_s = jnp.where(l > 0.0, acc / jnp.where(l > 0.0, l, 1.0), 0.0)
    gate = jax.nn.sigmoid(gate_ref[0])
    o_ref[0] = ocw_ref[0] + gate[:, 1:2] * o_s


def decode_attn_b(page_table_flat, idx_flat, q, cache, new_rows, gates, ocw, *, pages_per_seq, pos):
    s = q.shape[0]
    assert pos >= pages_per_seq * PAGE_SIZE - 1
    seq_spec = lambda shape: pl.BlockSpec((1,) + shape, lambda i, pt, idx: (i,) + (0,) * len(shape))
    return pl.pallas_call(
        functools.partial(_decode_attn_b_kernel, pages_per_seq=pages_per_seq, pos=pos),
        grid_spec=pltpu.PrefetchScalarGridSpec(
            num_scalar_prefetch=2,
            grid=(s,),
            in_specs=[
                seq_spec((N_HEADS, HEAD_DIM)),
                pl.BlockSpec(memory_space=pl.ANY),
                seq_spec((6, N_HEADS, HEAD_DIM)),
                seq_spec((N_HEADS, 3)),
                seq_spec((N_HEADS, HEAD_DIM)),
            ],
            out_specs=seq_spec((N_HEADS, HEAD_DIM)),
            scratch_shapes=[
                pltpu.VMEM((2, SEL_SLOTS, CMP_BLOCK, HEAD_DIM), F32),
                pltpu.VMEM((2, SEL_SLOTS, CMP_BLOCK, HEAD_DIM), F32),
                pltpu.SemaphoreType.DMA((2, 2)),
            ],
        ),
        out_shape=jax.ShapeDtypeStruct((s, N_HEADS, HEAD_DIM), F32),
        compiler_params=pltpu.CompilerParams(
            dimension_semantics=("arbitrary",), vmem_limit_bytes=VMEM_LIMIT_MID),
        name="decode_attn_b",
    )(page_table_flat, idx_flat, q, cache, new_rows, gates, ocw)


def nsa_decode_group(pn, row0, s, cache_nsa_kv, state_nsa_win, page_table, cmp_k, cmp_v):
    pages_per_seq = page_table.shape[1]
    past = pages_per_seq * PAGE_SIZE
    nb_dec = past // CMP_BLOCK + 1
    proj = Q_DIM + 6 * KV_DIM
    cos_s, sin_s = rope_tables(jnp.full((s,), past, jnp.int32))
    q_s, rows_s, win_s = nsa_layout(pn, cos_s, sin_s, n_seq=1, seq_len=s, row0=row0, tq=s)
    q_s = q_s[0].transpose(1, 0, 2)
    new_rows = jnp.concatenate([rows_s[0, 0], win_s[0]], axis=0).transpose(2, 0, 1, 3)
    new16 = jnp.repeat(new_rows, HPG, axis=2)
    pt_flat = page_table.reshape(-1)
    cache5 = cache_nsa_kv.reshape(cache_nsa_kv.shape[0], 4, PAGE_ROWS, CMP_BLOCK, HEAD_DIM)

    def summaries(comp, pe, w1, w2):
        past_c = paged_compress(pt_flat, cache5, pe, w1, w2, comp=comp)
        past_c = past_c.reshape(s, pages_per_seq, N_KV_HEADS, BLOCKS_PER_PAGE, HEAD_DIM)
        past_c = past_c.transpose(0, 2, 1, 3, 4).reshape(s, N_KV_HEADS, pages_per_seq * BLOCKS_PER_PAGE, HEAD_DIM)
        new_blk = jnp.pad(new_rows[:, comp].reshape(s * N_KV_HEADS, 1, HEAD_DIM), ((0, 0), (0, CMP_BLOCK - 1), (0, 0)))
        new_c = compress_blocks(new_blk, pe, w1, w2, tr=s * N_KV_HEADS, blocks=[0])
        both = jnp.concatenate([past_c, new_c.reshape(s, N_KV_HEADS, 1, HEAD_DIM)], axis=2)
        return jnp.pad(both, ((0, 0), (0, 0), (0, DEC_NB_PAD - nb_dec), (0, 0)))

    kc_s = summaries(0, *cmp_k)
    vc_s = summaries(1, *cmp_v)
    gates_s = pn[row0:row0 + s, proj:proj + 3 * N_HEADS].reshape(s, N_HEADS, 3)
    ocw, idx, win_out = decode_attn_a(q_s, kc_s, vc_s, new16, gates_s, state_nsa_win, nb=nb_dec, pos=past)
    idx_flat = idx[:, :, :N_KV_HEADS].transpose(0, 2, 1).reshape(-1)
    o_s = decode_attn_b(pt_flat, idx_flat, q_s, cache5, new16, gates_s, ocw, pages_per_seq=pages_per_seq, pos=past)
    return o_s.reshape(s, Q_DIM), new_rows[:, None, :4, :, None, :], win_out


def _matmul_residual_kernel(a_ref, a2_ref, w_ref, x_ref, o_ref, *, first_tiles):
    i = pl.program_id(0)

    @pl.when(i < first_tiles)
    def _():
        o_ref[...] = x_ref[...] + jnp.dot(a_ref[...].astype(BF16), w_ref[...], preferred_element_type=F32)

    @pl.when(i >= first_tiles)
    def _():
        o_ref[...] = x_ref[...] + jnp.dot(a2_ref[...].astype(BF16), w_ref[...], preferred_element_type=F32)


def matmul_residual(a, a2, w, x, *, tm):
    n1, k = a.shape
    n2 = a2.shape[0]
    n, d = x.shape
    assert n == n1 + n2 and n1 % tm == 0 and n2 % tm == 0
    first = n1 // tm
    return pl.pallas_call(
        functools.partial(_matmul_residual_kernel, first_tiles=first),
        grid=(n // tm,),
        in_specs=[
            pl.BlockSpec((tm, k), lambda i: (jnp.minimum(i, first - 1), 0)),
            pl.BlockSpec((tm, k), lambda i: (jnp.maximum(i - first, 0), 0)),
            pl.BlockSpec((k, d), lambda i: (0, 0)),
            pl.BlockSpec((tm, d), lambda i: (i, 0)),
        ],
        out_specs=pl.BlockSpec((tm, d), lambda i: (i, 0)),
        out_shape=jax.ShapeDtypeStruct((n, d), F32),
        compiler_params=pltpu.CompilerParams(
            dimension_semantics=("parallel",), vmem_limit_bytes=VMEM_LIMIT_MID),
        name="matmul_residual",
    )(a, a2, w, x)


def _rmsnorm_kernel(x_ref, g_ref, o_ref):
    x = x_ref[...]
    ms = jnp.mean(x * x, axis=-1, keepdims=True)
    o_ref[...] = x * lax.rsqrt(ms + RMS_EPS) * g_ref[...]


def rmsnorm_rows(x, g, *, row0, n_rows, tm):
    d = x.shape[1]
    assert row0 % tm == 0 and n_rows % tm == 0
    blk0 = row0 // tm
    return pl.pallas_call(
        _rmsnorm_kernel,
        grid=(n_rows // tm,),
        in_specs=[
            pl.BlockSpec((tm, d), lambda i: (blk0 + i, 0)),
            pl.BlockSpec((1, d), lambda i: (0, 0)),
        ],
        out_specs=pl.BlockSpec((tm, d), lambda i: (i, 0)),
        out_shape=jax.ShapeDtypeStruct((n_rows, d), F32),
        compiler_params=pltpu.CompilerParams(dimension_semantics=("parallel",)),
        name="rmsnorm_rows",
    )(x, g.reshape(1, d))


def kernel(x_prompt, x_sample, state_conv, cache_nsa_kv, state_nsa_win, page_table, norm_mix, norm_ffn, norm_final, conv_w_in, conv_w, conv_w_out, nsa_w_in, nsa_w_out, nsa_cmp_pe_k, nsa_cmp_w1_k, nsa_cmp_w2_k, nsa_cmp_pe_v, nsa_cmp_w1_v, nsa_cmp_w2_v, peer_w_q, peer_sub_keys, peer_u, peer_v):
    b, t, d = x_prompt.shape
    s = x_sample.shape[0]
    assert x_sample.shape[1] == 1 and d == D_MODEL
    n_prompt = b * t
    n_dec = DEC_ROWS

    u_all = peer_u.astype(BF16)
    vt_all = peer_v.astype(BF16).transpose(0, 2, 1)

    def peer(x, i):
        return peer_layer(
            x, norm_ffn[i], peer_w_q[i].astype(BF16),
            peer_sub_keys[i].reshape(2 * PEER_HEADS, PEER_KEYS, PEER_HALF_DIM).astype(BF16),
            u_all, vt_all, i)

    x = jnp.concatenate([x_prompt.reshape(n_prompt, d), x_sample.reshape(s, d),
                         jnp.zeros((n_dec - s, d), F32)], axis=0)

    p = rms_matmul(x, norm_mix[0], conv_w_in[0].astype(BF16), tm=512, tn=1024)
    st = state_conv[:, 0]
    pad_dec = lambda a: jnp.pad(a, ((0, n_dec - s), (0, 0)))
    x, tail, u_dec = conv_core(p, x, pad_dec(st[:, 0]), pad_dec(st[:, 1]), conv_w[0],
                               conv_w_out[0].astype(BF16), n_prompt=n_prompt, seq_len=t)
    tail_p = tail[:n_prompt // CONV_TM].reshape(b, t // CONV_TM, CONV_TAIL, d)
    conv_state_prompt = tail_p[:, -1, CONV_TAIL - 2:][:, None]
    conv_state_sample = jnp.stack([st[:, 1], u_dec[:s]], axis=1)[:, None]
    x = peer(x, 0)

    proj = Q_DIM + 6 * KV_DIM
    n_gate = 3 * N_HEADS
    w_in = jnp.pad(nsa_w_in[0], ((0, 0), (0, NSA_PROJ_PAD - proj - n_gate))).astype(BF16)
    pn = rms_matmul(x, norm_mix[1], w_in, tm=512, tn=512)
    gates = pn[:, proj:proj + n_gate]
    w1_shape = (CMP_PASSES, CMP_PAIR * HEAD_DIM, CMP_HIDDEN)
    w1k, w2k = nsa_cmp_w1_k[0].astype(BF16).reshape(w1_shape), nsa_cmp_w2_k[0].astype(BF16)
    w1v, w2v = nsa_cmp_w1_v[0].astype(BF16).reshape(w1_shape), nsa_cmp_w2_v[0].astype(BF16)
    pe_k, pe_v = nsa_cmp_pe_k[0], nsa_cmp_pe_v[0]

    cos_p, sin_p = rope_tables(jnp.arange(t))
    q_p, rows_p, win_p = nsa_layout(pn, cos_p, sin_p, n_seq=b, seq_len=t, row0=0, tq=ATT_TQ)
    nb = t // CMP_BLOCK
    flat = rows_p.reshape(b * 4 * N_KV_HEADS * nb, CMP_BLOCK, HEAD_DIM)
    tr = N_KV_HEADS * nb
    kc_p = compress_blocks(flat, pe_k, w1k, w2k, tr=tr, blocks=[4 * i for i in range(b)])
    vc_p = compress_blocks(flat, pe_v, w1v, w2v, tr=tr, blocks=[4 * i + 1 for i in range(b)])
    gates_p = gates[:n_prompt].reshape(b, t, N_KV_HEADS, 3 * HPG).transpose(0, 2, 1, 3)
    o_p = nsa_prompt_attn(q_p, kc_p.reshape(b, N_KV_HEADS, nb, HEAD_DIM), vc_p.reshape(b, N_KV_HEADS, nb, HEAD_DIM),
                          rows_p, win_p, gates_p)
    wb_p = min(WINDOW, t)
    nsa_win_prompt = win_p[:, None, :, :, t - wb_p:, :]

    o_s, nsa_rows_sample, nsa_win_sample = nsa_decode_group(
        pn, n_prompt, s, cache_nsa_kv, state_nsa_win, page_table, (pe_k, w1k, w2k), (pe_v, w1v, w2v))
    o_s = jnp.pad(o_s, ((0, n_dec - s), (0, 0)))

    x = matmul_residual(o_p, o_s, nsa_w_out[0].astype(BF16), x, tm=256)
    x = peer(x, 1)

    y_prompt = rmsnorm_rows(x, norm_final, row0=0, n_rows=n_prompt, tm=512).reshape(b, t, d)
    y_sample = rmsnorm_rows(x, norm_final, row0=n_prompt, n_rows=s, tm=s).reshape(s, 1, d)
    return (y_prompt, y_sample, conv_state_prompt, conv_state_sample,
            rows_p, nsa_rows_sample, nsa_win_prompt, nsa_win_sample)
```

```python
import functools
import math

import jax
import jax.numpy as jnp
from jax import lax
from jax.experimental import pallas as pl
from jax.experimental.pallas import tpu as pltpu

F32 = jnp.float32
BF16 = jnp.bfloat16

D_MODEL = 2048
RMS_EPS = 1e-6

PEER_HEADS = 8
PEER_KEYS = 128
PEER_TOPK = 16
PEER_HALF_DIM = 128
PEER_EXPERTS = PEER_KEYS * PEER_KEYS

VMEM_LIMIT_BIG = 60 * 1024 * 1024
VMEM_LIMIT_MID = 48 * 1024 * 1024

NT_DIMS = (((1,), (1,)), ((), ()))


def _gelu_tanh(x):
    c = math.sqrt(2.0 / math.pi)
    half = 0.5 * x
    return half + half * jnp.tanh(x * (c + (c * 0.044715) * (x * x)))


def _rms_matmul_kernel(x_ref, g_ref, w_ref, o_ref, *rest, emit_ht):
    if emit_ht:
        ht_ref, h_scr = rest
    else:
        (h_scr,) = rest

    @pl.when(pl.program_id(1) == 0)
    def _():
        x = x_ref[...]
        ms = jnp.mean(x * x, axis=-1, keepdims=True)
        h = x * lax.rsqrt(ms + RMS_EPS) * g_ref[...]
        h_scr[...] = h.astype(BF16)
        if emit_ht:
            ht_ref[...] = h.T.astype(BF16)

    o_ref[...] = jnp.dot(h_scr[...], w_ref[...], preferred_element_type=F32)


def _rms_matmul_wide_kernel(x_ref, g_ref, w_ref, o_ref):
    x = x_ref[...]
    ms = jnp.mean(x * x, axis=-1, keepdims=True)
    h = (x * lax.rsqrt(ms + RMS_EPS) * g_ref[...]).astype(BF16)
    o_ref[...] = jnp.dot(h, w_ref[...], preferred_element_type=F32)


def rms_matmul_wide(x, g, w, *, tm, tn):
    n, d = x.shape
    m = w.shape[1]
    assert n % tm == 0 and m % tn == 0
    return pl.pallas_call(
        _rms_matmul_wide_kernel,
        grid=(m // tn, n // tm),
        in_specs=[
            pl.BlockSpec((tm, d), lambda j, i: (i, 0)),
            pl.BlockSpec((1, d), lambda j, i: (0, 0)),
            pl.BlockSpec((d, tn), lambda j, i: (0, j)),
        ],
        out_specs=pl.BlockSpec((tm, tn), lambda j, i: (i, j)),
        out_shape=jax.ShapeDtypeStruct((n, m), F32),
        compiler_params=pltpu.CompilerParams(
            dimension_semantics=("arbitrary", "parallel"), vmem_limit_bytes=VMEM_LIMIT_BIG),
        name="rms_matmul_wide",
    )(x, g.reshape(1, d), w)


def rms_matmul(x, g, w, *, tm, tn, emit_ht=False):
    n, d = x.shape
    m = w.shape[1]
    assert n % tm == 0 and m % tn == 0
    out_shape = [jax.ShapeDtypeStruct((n, m), F32)]
    out_specs = [pl.BlockSpec((tm, tn), lambda i, j: (i, j))]
    if emit_ht:
        out_shape.append(jax.ShapeDtypeStruct((d, n), BF16))
        out_specs.append(pl.BlockSpec((d, tm), lambda i, j: (0, i)))
    res = pl.pallas_call(
        functools.partial(_rms_matmul_kernel, emit_ht=emit_ht),
        grid=(n // tm, m // tn),
        in_specs=[
            pl.BlockSpec((tm, d), lambda i, j: (i, 0)),
            pl.BlockSpec((1, d), lambda i, j: (0, 0)),
            pl.BlockSpec((d, tn), lambda i, j: (0, j)),
        ],
        out_specs=out_specs,
        out_shape=out_shape,
        scratch_shapes=[pltpu.VMEM((tm, d), BF16)],
        compiler_params=pltpu.CompilerParams(
            dimension_semantics=("parallel", "arbitrary"), vmem_limit_bytes=VMEM_LIMIT_MID),
        name="rms_matmul_ht" if emit_ht else "rms_matmul",
    )(x, g.reshape(1, d), w)
    return res if emit_ht else res[0]


NEG_BIG = -jnp.inf
NOT_RANKED = 99.0


def _top16_rows(s, break_ties):
    rows = lax.broadcasted_iota(jnp.int32, s.shape, 0).astype(F32)
    rank = jnp.full(s.shape, NOT_RANKED, F32)
    vals = []
    for k in range(PEER_TOPK):
        m = jnp.max(s, axis=0, keepdims=True)
        if break_ties:
            idx = jnp.min(jnp.where(s == m, rows, float(PEER_KEYS)), axis=0, keepdims=True)
            hit = rows == idx
        else:
            hit = s == m
        rank = jnp.where(hit, float(k), rank)
        s = jnp.where(hit, NEG_BIG, s)
        vals.append(m)
    return vals, rank


def _pair_counts(a, b, break_ties):
    t = a[0].shape[1]
    b16 = jnp.concatenate(b, axis=0)
    a_hi = jnp.concatenate(a[8:], axis=0)
    sub8 = lax.broadcasted_iota(jnp.int32, (8, t), 0).astype(F32)
    sub16 = lax.broadcasted_iota(jnp.int32, (16, t), 0).astype(F32)
    cands = [a[0] + b16]
    flats = [sub16]
    for k0 in range(1, 8):
        lim = PEER_TOPK // (k0 + 1)
        c = a[k0] + b16[:8]
        if lim < 8:
            c = jnp.where(sub8 < float(lim), c, NEG_BIG)
        cands.append(c)
        flats.append(sub8 + float(k0 * 16))
    cands.append(a_hi + b[0])
    flats.append((sub8 + 8.0) * 16.0)

    cmax = a[0] + b[0]
    counts = [jnp.zeros(c.shape, F32) for c in cands]
    z = jnp.zeros((1, t), F32)
    for _ in range(PEER_TOPK):
        m = cands[0].max(axis=0, keepdims=True)
        for c in cands[1:]:
            m = jnp.maximum(m, c.max(axis=0, keepdims=True))
        if break_ties:
            idx = None
            for c, f in zip(cands, flats):
                i = jnp.min(jnp.where(c == m, f, 999.0), axis=0, keepdims=True)
                idx = i if idx is None else jnp.minimum(idx, i)
            hits = [f == idx for f in flats]
        else:
            hits = [c == m for c in cands]
        for j, hit in enumerate(hits):
            counts[j] = counts[j] + hit.astype(F32)
            cands[j] = jnp.where(hit, NEG_BIG, cands[j])
        z = z + jnp.exp(m - cmax)
    n = [c.sum(axis=0, keepdims=True) for c in counts[:8]]
    n += [counts[8][r:r + 1] for r in range(8)]
    return n, z


def _peer_route_kernel(q_ref, keys_ref, r1_ref, cnt_ref, a_ref, b_ref):
    def head(h, carry):
        col = pl.multiple_of(h * (2 * PEER_HALF_DIM), 2 * PEER_HALF_DIM)
        q0 = q_ref[:, pl.ds(col, PEER_HALF_DIM)].astype(BF16)
        q1 = q_ref[:, pl.ds(col + PEER_HALF_DIM, PEER_HALF_DIM)].astype(BF16)
        s0 = lax.dot_general(keys_ref[2 * h], q0, NT_DIMS, preferred_element_type=F32)
        s1 = lax.dot_general(keys_ref[2 * h + 1], q1, NT_DIMS, preferred_element_type=F32)

        def route(break_ties):
            v0, rank0 = _top16_rows(s0, break_ties)
            v1, rank1 = _top16_rows(s1, break_ties)
            n, z = _pair_counts(v0, v1, break_ties)
            cnt = jnp.zeros(s0.shape, F32)
            for k0 in range(PEER_TOPK):
                cnt = jnp.where(rank0 == float(k0), n[k0], cnt)
            r1_ref[h] = rank1.astype(r1_ref.dtype)
            cnt_ref[h] = cnt
            a_ref[h] = jnp.exp(s0 - v0[0])
            b_ref[h] = (jnp.exp(s1 - v1[0]) / z).astype(b_ref.dtype)
            ranked = (jnp.sum(jnp.where(rank0 < NOT_RANKED, 1.0, 0.0), axis=0, keepdims=True)
                      + jnp.sum(jnp.where(rank1 < NOT_RANKED, 1.0, 0.0), axis=0, keepdims=True))
            pairs = n[0]
            for nk in n[1:]:
                pairs = pairs + nk
            clean = jnp.logical_and(ranked == float(2 * PEER_TOPK), pairs == float(PEER_TOPK))
            return jnp.min(jnp.where(clean, 1.0, 0.0))

        all_clean = route(break_ties=False)

        @pl.when(all_clean < 0.5)
        def _():
            route(break_ties=True)

        return carry

    lax.fori_loop(0, PEER_HEADS, head, 0)


def peer_route(q, keys, *, tm):
    n = q.shape[0]
    assert n % tm == 0
    sds = [jax.ShapeDtypeStruct((PEER_HEADS, PEER_KEYS, n), dt) for dt in (BF16, F32, F32, BF16)]
    ospec = pl.BlockSpec((PEER_HEADS, PEER_KEYS, tm), lambda i: (0, 0, i))
    return pl.pallas_call(
        _peer_route_kernel,
        grid=(n // tm,),
        in_specs=[
            pl.BlockSpec((tm, PEER_HEADS * 2 * PEER_HALF_DIM), lambda i: (i, 0)),
            pl.BlockSpec((2 * PEER_HEADS, PEER_KEYS, PEER_HALF_DIM), lambda i: (0, 0, 0)),
        ],
        out_specs=[ospec] * 4,
        out_shape=sds,
        compiler_params=pltpu.CompilerParams(
            dimension_semantics=("parallel",), vmem_limit_bytes=VMEM_LIMIT_MID),
        name="peer_route",
    )(q, keys)


PEER_TE = 1024
PEER_ROWS = PEER_TE // PEER_KEYS


PEER_CHUNK = 16


def _peer_gate_block(act_ref, p_ref, r1_ref, b_ref, cnt_ref, a_ref, r, tc):
    cols = slice(tc * 128, (tc + 1) * 128)
    n_chunks = PEER_KEYS // PEER_CHUNK
    w = [jnp.zeros((PEER_CHUNK, 128), BF16) for _ in range(n_chunks)]
    zero = jnp.zeros((), BF16)
    for h in range(PEER_HEADS):
        cnt = jnp.broadcast_to(cnt_ref[h, 0, r:r + 1, cols], (PEER_CHUNK, 128)).astype(BF16)
        a = jnp.broadcast_to(a_ref[h, 0, r:r + 1, cols], (PEER_CHUNK, 128)).astype(BF16)
        for c in range(n_chunks):
            rows = slice(c * PEER_CHUNK, (c + 1) * PEER_CHUNK)
            first = jnp.minimum(jnp.maximum(cnt - r1_ref[h, rows, cols], zero), a)
            w[c] = w[c] + first * b_ref[h, rows, cols]
    for c in range(n_chunks):
        rows = slice(r * PEER_KEYS + c * PEER_CHUNK, r * PEER_KEYS + (c + 1) * PEER_CHUNK)
        g = _gelu_tanh(act_ref[rows, cols])
        p_ref[rows, cols] = (w[c].astype(F32) * g).astype(BF16)


def _peer_main_kernel(ht_ref, r1_ref, b_ref, cnt_ref, a_ref, u_ref, vt_ref, x_ref, o_ref,
                      acc_ref, act0_ref, act1_ref, p0_ref, p1_ref, *, tm, n_e):
    s = pl.program_id(1)
    acts = (act0_ref, act1_ref)
    ps = (p0_ref, p1_ref)

    def act_piece(parity, mh, nh):
        rows = slice(mh * (PEER_TE // 2), (mh + 1) * (PEER_TE // 2))
        cols = slice(nh * (tm // 2), (nh + 1) * (tm // 2))
        acts[parity][rows, cols] = jnp.dot(u_ref[rows, :], ht_ref[:, cols], preferred_element_type=F32)

    def out_piece(parity, mh, nh):
        d = acc_ref.shape[0]
        rows = slice(mh * (d // 2), (mh + 1) * (d // 2))
        cols = slice(nh * (tm // 2), (nh + 1) * (tm // 2))
        acc_ref[rows, cols] += jnp.dot(vt_ref[rows, :], ps[parity][:, cols], preferred_element_type=F32)

    def step(parity, do_act, do_gates, do_out):
        halves = [(mh, nh) for mh in range(2) for nh in range(2)]
        pieces = [functools.partial(act_piece, parity, *h) for h in halves] if do_act else []
        pieces += [functools.partial(out_piece, parity, *h) for h in halves] if do_out else []
        blocks = [(r, tc) for r in range(PEER_ROWS) for tc in range(tm // 128)] if do_gates else []
        per = -(-len(blocks) // max(len(pieces), 1))
        for i, piece in enumerate(pieces):
            piece()
            for r, tc in blocks[i * per:(i + 1) * per]:
                _peer_gate_block(acts[1 - parity], ps[1 - parity], r1_ref, b_ref, cnt_ref, a_ref, r, tc)
        for r, tc in blocks[len(pieces) * per:]:
            _peer_gate_block(acts[1 - parity], ps[1 - parity], r1_ref, b_ref, cnt_ref, a_ref, r, tc)

    @pl.when(s == 0)
    def _():
        acc_ref[...] = jnp.zeros_like(acc_ref)
        step(0, True, False, False)

    @pl.when(s == 1)
    def _():
        step(1, True, True, False)

    for parity in range(2):
        @pl.when(jnp.logical_and(jnp.logical_and(s >= 2, s < n_e), s % 2 == parity))
        def _():
            step(parity, True, True, True)

    @pl.when(s == n_e)
    def _():
        step(n_e % 2, False, True, True)

    @pl.when(s == n_e + 1)
    def _():
        step((n_e + 1) % 2, False, False, True)
        o_ref[...] = x_ref[...] + acc_ref[...].T


def peer_main(ht, route, u_all, vt_all, layer, x, *, tm):
    d, n = ht.shape
    rank1, cnt0, a, b = route
    n_e = PEER_EXPERTS // PEER_TE
    rows_shape = (PEER_HEADS, n_e, PEER_ROWS, n)
    clamp = lambda v: jnp.clip(v, 0, n_e - 1)
    full_spec = pl.BlockSpec((PEER_HEADS, PEER_KEYS, tm), lambda i, s: (0, 0, i))
    rows_spec = pl.BlockSpec((PEER_HEADS, 1, PEER_ROWS, tm), lambda i, s: (0, clamp(s - 1), 0, i))
    return pl.pallas_call(
        functools.partial(_peer_main_kernel, tm=tm, n_e=n_e),
        grid=(n // tm, n_e + 2),
        in_specs=[
            pl.BlockSpec((d, tm), lambda i, s: (0, i)),
            full_spec, full_spec, rows_spec, rows_spec,
            pl.BlockSpec((None, PEER_TE, d), lambda i, s: (layer, clamp(s), 0)),
            pl.BlockSpec((None, d, PEER_TE), lambda i, s: (layer, 0, clamp(s - 2))),
            pl.BlockSpec((tm, d), lambda i, s: (i, 0)),
        ],
        out_specs=pl.BlockSpec((tm, d), lambda i, s: (i, 0)),
        out_shape=jax.ShapeDtypeStruct((n, d), F32),
        scratch_shapes=[
            pltpu.VMEM((d, tm), F32),
            pltpu.VMEM((PEER_TE, tm), F32),
            pltpu.VMEM((PEER_TE, tm), F32),
            pltpu.VMEM((PEER_TE, tm), BF16),
            pltpu.VMEM((PEER_TE, tm), BF16),
        ],
        compiler_params=pltpu.CompilerParams(
            dimension_semantics=("parallel", "arbitrary"), vmem_limit_bytes=VMEM_LIMIT_BIG),
        name="peer_main",
    )(ht, rank1, b, cnt0.reshape(rows_shape), a.reshape(rows_shape), u_all, vt_all, x)


def peer_layer(x, g, w_q, keys, u_all, vt_all, layer, *, tm_q=512, tm_route=256, tm_main=512):
    q, ht = rms_matmul(x, g, w_q, tm=tm_q, tn=1024, emit_ht=True)
    route = peer_route(q, keys, tm=tm_route)
    return peer_main(ht, route, u_all, vt_all, layer, x, tm=tm_main)


CONV_TM = 256
CONV_TAIL = 8


def _conv_core_kernel(b_ref, c_ref, h_ref, x_ref, s0_ref, s1_ref, cw_ref, wout_ref,
                      o_ref, tail_ref, us_ref, ubuf, *, prompt_tiles, tiles_per_seq):
    i = pl.program_id(0)
    tm = CONV_TM
    u = c_ref[...] * h_ref[...]
    w0 = cw_ref[0:1, :]
    w1 = cw_ref[1:2, :]
    w2 = cw_ref[2:3, :]

    @pl.when(i % tiles_per_seq == 0)
    def _():
        ubuf[0:CONV_TAIL, :] = jnp.zeros((CONV_TAIL, D_MODEL), F32)

    ubuf[CONV_TAIL:CONV_TAIL + tm, :] = u
    tail_ref[0] = u[tm - CONV_TAIL:, :]

    @pl.when(i < prompt_tiles)
    def _():
        z = ubuf[CONV_TAIL - 2:CONV_TAIL - 2 + tm, :] * w0 + ubuf[CONV_TAIL - 1:CONV_TAIL - 1 + tm, :] * w1 + u * w2
        y = jnp.dot((b_ref[...] * z).astype(BF16), wout_ref[...], preferred_element_type=F32)
        o_ref[...] = x_ref[...] + y
        ubuf[0:CONV_TAIL, :] = u[tm - CONV_TAIL:, :]

    @pl.when(i >= prompt_tiles)
    def _():
        z = s0_ref[...] * w0 + s1_ref[...] * w1 + u * w2
        y = jnp.dot((b_ref[...] * z).astype(BF16), wout_ref[...], preferred_element_type=F32)
        o_ref[...] = x_ref[...] + y
        us_ref[...] = u


def conv_core(p, x, s0, s1, cw, wout, *, n_prompt, seq_len):
    n, d = x.shape
    tm = CONV_TM
    n_tiles = n // tm
    prompt_tiles = n_prompt // tm
    tiles_per_seq = seq_len // tm
    n_dec = n - n_prompt
    dec_map = lambda i: (jnp.maximum(i - prompt_tiles, 0), 0)
    return pl.pallas_call(
        functools.partial(_conv_core_kernel, prompt_tiles=prompt_tiles, tiles_per_seq=tiles_per_seq),
        grid=(n_tiles,),
        in_specs=[
            pl.BlockSpec((tm, d), lambda i: (i, 0)),
            pl.BlockSpec((tm, d), lambda i: (i, 1)),
            pl.BlockSpec((tm, d), lambda i: (i, 2)),
            pl.BlockSpec((tm, d), lambda i: (i, 0)),
            pl.BlockSpec((tm, d), dec_map),
            pl.BlockSpec((tm, d), dec_map),
            pl.BlockSpec((3, d), lambda i: (0, 0)),
            pl.BlockSpec((d, d), lambda i: (0, 0)),
        ],
        out_specs=[
            pl.BlockSpec((tm, d), lambda i: (i, 0)),
            pl.BlockSpec((1, CONV_TAIL, d), lambda i: (i, 0, 0)),
            pl.BlockSpec((tm, d), dec_map),
        ],
        out_shape=[
            jax.ShapeDtypeStruct((n, d), F32),
            jax.ShapeDtypeStruct((n_tiles, CONV_TAIL, d), F32),
            jax.ShapeDtypeStruct((n_dec, d), F32),
        ],
        scratch_shapes=[pltpu.VMEM((CONV_TAIL + tm, d), F32)],
        compiler_params=pltpu.CompilerParams(
            dimension_semantics=("arbitrary",), vmem_limit_bytes=VMEM_LIMIT_BIG),
        name="conv_core",
    )(p, p, p, x, s0, s1, cw, wout)


N_HEADS = 16
N_KV_HEADS = 4
HPG = N_HEADS // N_KV_HEADS
HEAD_DIM = 128
Q_DIM = N_HEADS * HEAD_DIM
KV_DIM = N_KV_HEADS * HEAD_DIM
CMP_BLOCK = 64
N_SELECT = 16
WINDOW = 512
ROPE_THETA = 10000.0
NEG_INF = -1e30
FORCED_SCORE = 1e6


def _rope(x, cos, sin_signed):
    return x * cos + pltpu.roll(x, HEAD_DIM // 2, 1) * sin_signed


def _nsa_layout_kernel(p_ref, cos_ref, sin_ref, q_ref, rows_ref, win_ref):
    cos = cos_ref[...]
    sin = sin_ref[...]
    scale = HEAD_DIM ** -0.5
    for h in range(N_HEADS):
        q_ref[0, h] = (_rope(p_ref[:, h * HEAD_DIM:(h + 1) * HEAD_DIM], cos, sin) * scale).astype(BF16)
    for c in range(6):
        for g in range(N_KV_HEADS):
            col = Q_DIM + c * KV_DIM + g * HEAD_DIM
            v = p_ref[:, col:col + HEAD_DIM]
            if c % 2 == 0:
                v = _rope(v, cos, sin)
            if c < 4:
                rows_ref[0, 0, c, g] = v
            else:
                win_ref[0, c - 4, g] = v


def nsa_layout(p, cos, sin, *, n_seq, seq_len, row0, tq):
    width = Q_DIM + 6 * KV_DIM
    n_t = seq_len // tq
    blk0 = row0 // tq
    assert row0 % tq == 0
    return pl.pallas_call(
        _nsa_layout_kernel,
        grid=(n_seq, n_t),
        in_specs=[
            pl.BlockSpec((tq, width), lambda b, t: (blk0 + b * n_t + t, 0)),
            pl.BlockSpec((tq, HEAD_DIM), lambda b, t: (t, 0)),
            pl.BlockSpec((tq, HEAD_DIM), lambda b, t: (t, 0)),
        ],
        out_specs=[
            pl.BlockSpec((1, N_HEADS, tq, HEAD_DIM), lambda b, t: (b, 0, t, 0)),
            pl.BlockSpec((1, 1, 4, N_KV_HEADS, tq, HEAD_DIM), lambda b, t: (b, 0, 0, 0, t, 0)),
            pl.BlockSpec((1, 2, N_KV_HEADS, tq, HEAD_DIM), lambda b, t: (b, 0, 0, t, 0)),
        ],
        out_shape=[
            jax.ShapeDtypeStruct((n_seq, N_HEADS, seq_len, HEAD_DIM), BF16),
            jax.ShapeDtypeStruct((n_seq, 1, 4, N_KV_HEADS, seq_len, HEAD_DIM), F32),
            jax.ShapeDtypeStruct((n_seq, 2, N_KV_HEADS, seq_len, HEAD_DIM), F32),
        ],
        compiler_params=pltpu.CompilerParams(
            dimension_semantics=("parallel", "parallel"), vmem_limit_bytes=VMEM_LIMIT_MID),
        name="nsa_layout",
    )(p, cos, sin)


def rope_tables(pos):
    half = HEAD_DIM // 2
    inv = ROPE_THETA ** (-jnp.arange(half, dtype=F32) / half)
    ang = pos.astype(F32)[:, None] * inv[None, :]
    cos = jnp.cos(ang)
    sin = jnp.sin(ang)
    return jnp.concatenate([cos, cos], axis=1), jnp.concatenate([-sin, sin], axis=1)


CMP_HIDDEN = 256
CMP_PAIR = 2
CMP_PASSES = CMP_BLOCK // CMP_PAIR


def _compress_rows(position_rows, n_rows, pe_ref, w1_ref, w2_ref):
    acc = jnp.zeros((n_rows, CMP_HIDDEN), F32)
    for p in range(CMP_PASSES):
        pair = [position_rows(CMP_PAIR * p + j) + pe_ref[CMP_PAIR * p + j:CMP_PAIR * p + j + 1, :]
                for j in range(CMP_PAIR)]
        kb = jnp.concatenate(pair, axis=1).astype(BF16)
        acc = acc + jnp.dot(kb, w1_ref[p], preferred_element_type=F32)
    hid = _gelu_tanh(acc)
    return jnp.dot(hid.astype(BF16), w2_ref[...], preferred_element_type=F32)


def _compress_kernel(blk_ref, x_ref, pe_ref, w1_ref, w2_ref, o_ref):
    del blk_ref
    o_ref[...] = _compress_rows(lambda l: x_ref[:, l, :], x_ref.shape[0], pe_ref, w1_ref, w2_ref)


def _compress_weight_specs():
    const = lambda n: (lambda *args: (0,) * n)
    return [
        pl.BlockSpec((CMP_BLOCK, HEAD_DIM), const(2)),
        pl.BlockSpec((CMP_PASSES, CMP_PAIR * HEAD_DIM, CMP_HIDDEN), const(3)),
        pl.BlockSpec((CMP_HIDDEN, HEAD_DIM), const(2)),
    ]


def compress_blocks(x, pe, w1, w2, *, tr, blocks):
    blocks = jnp.asarray(blocks, jnp.int32)
    nb = blocks.shape[0]
    return pl.pallas_call(
        _compress_kernel,
        grid_spec=pltpu.PrefetchScalarGridSpec(
            num_scalar_prefetch=1,
            grid=(nb,),
            in_specs=[pl.BlockSpec((tr, CMP_BLOCK, HEAD_DIM), lambda i, blk: (blk[i], 0, 0))]
            + _compress_weight_specs(),
            out_specs=pl.BlockSpec((tr, HEAD_DIM), lambda i, blk: (i, 0)),
        ),
        out_shape=jax.ShapeDtypeStruct((nb * tr, HEAD_DIM), F32),
        compiler_params=pltpu.CompilerParams(
            dimension_semantics=("arbitrary",), vmem_limit_bytes=VMEM_LIMIT_MID),
        name="compress_blocks",
    )(blocks, x, pe, w1, w2)


ATT_TQ = 256
ATT_TK = 512


def _softmax_masked(s, mask, axis):
    sm = jnp.where(mask, s, NEG_INF)
    e = jnp.exp(sm - jnp.max(sm, axis=axis, keepdims=True))
    p = e / jnp.sum(e, axis=axis, keepdims=True)
    return jnp.where(mask, p, 0.0)


def _block_ranks(score_t, n_real):
    blk = lax.broadcasted_iota(jnp.int32, score_t.shape, 0)
    rank = jnp.zeros(score_t.shape, F32)
    for m in range(n_real):
        row = score_t[m:m + 1, :]
        earlier = jnp.where(blk > m, 1.0, 0.0)
        rank = rank + jnp.where(row > score_t, 1.0, jnp.where(row == score_t, earlier, 0.0))
    return rank


def _select_blocks(score_t):
    return jnp.where(_block_ranks(score_t, score_t.shape[0]) < float(N_SELECT), 1.0, 0.0)


def _flash(q_ref, k_tile, v_tile, kt_lo, kt_hi, bias_of, m_scr, l_scr, acc_scr):
    m_scr[...] = jnp.full(m_scr.shape, NEG_INF, F32)
    l_scr[...] = jnp.zeros(l_scr.shape, F32)
    acc_scr[...] = jnp.zeros(acc_scr.shape, F32)
    q2 = q_ref[0].reshape(HPG * ATT_TQ, HEAD_DIM)

    def body(kt, carry):
        s = lax.dot_general(q2, k_tile(kt), NT_DIMS, preferred_element_type=F32).reshape(HPG, ATT_TQ, ATT_TK)
        s = s + bias_of(kt)[None]
        m_old = m_scr[...]
        m_new = jnp.maximum(m_old, jnp.max(s, axis=-1, keepdims=True))
        alpha = jnp.exp(m_old - m_new)
        p = jnp.exp(s - m_new)
        l_scr[...] = alpha * l_scr[...] + jnp.sum(p, axis=-1, keepdims=True)
        pv = jnp.dot(p.reshape(HPG * ATT_TQ, ATT_TK).astype(BF16), v_tile(kt), preferred_element_type=F32)
        acc_scr[...] = alpha * acc_scr[...] + pv.reshape(HPG, ATT_TQ, HEAD_DIM)
        m_scr[...] = m_new
        return carry

    lax.fori_loop(kt_lo, kt_hi, body, 0)
    return acc_scr[...] / l_scr[...]


def _nsa_prompt_attn_kernel(q_ref, kc_ref, vc_ref, ks_ref, vs_ref, kw_ref, vw_ref, gate_ref, o_ref,
                            m_scr, l_scr, acc_scr):
    tq = ATT_TQ
    qi = pl.program_id(2)
    q0 = qi * tq
    q3 = q_ref[0]
    q2 = q3.reshape(HPG * tq, HEAD_DIM)
    nb = kc_ref.shape[2]
    kc = kc_ref[0, 0].astype(BF16)
    vc = vc_ref[0, 0].astype(BF16)

    pos_col = q0 + lax.broadcasted_iota(jnp.int32, (tq, 1), 0)
    blk_row = lax.broadcasted_iota(jnp.int32, (1, nb), 1)
    c_mask = ((blk_row + 1) * CMP_BLOCK - 1) <= pos_col
    s_c = lax.dot_general(q2, kc, NT_DIMS, preferred_element_type=F32).reshape(HPG, tq, nb)
    p_c = _softmax_masked(s_c, c_mask[None], axis=-1)
    o_c = jnp.dot(p_c.reshape(HPG * tq, nb).astype(BF16), vc, preferred_element_type=F32).reshape(HPG, tq, HEAD_DIM)

    pos_row = q0 + lax.broadcasted_iota(jnp.int32, (1, tq), 1)
    blk_col = lax.broadcasted_iota(jnp.int32, (nb, 1), 0)
    c_mask_t = ((blk_col + 1) * CMP_BLOCK - 1) <= pos_row
    imp = jnp.zeros((nb, tq), F32)
    for h in range(HPG):
        s_t = lax.dot_general(kc, q3[h], NT_DIMS, preferred_element_type=F32)
        imp = imp + _softmax_masked(s_t, c_mask_t, axis=0)
    cur = pos_row // CMP_BLOCK
    reach = blk_col <= cur
    forced = jnp.logical_and(reach, jnp.logical_or(blk_col == 0, blk_col >= cur - 1))
    score = jnp.where(forced, FORCED_SCORE, jnp.where(reach, imp, NEG_INF))
    sel_t = _select_blocks(score).astype(BF16)
    eye = jnp.where(lax.broadcasted_iota(jnp.int32, (tq, tq), 0) == lax.broadcasted_iota(jnp.int32, (tq, tq), 1),
                    1.0, 0.0).astype(BF16)
    sel = lax.dot_general(eye, sel_t, NT_DIMS, preferred_element_type=F32).astype(BF16)

    def key_pos(kt):
        return kt * ATT_TK + lax.broadcasted_iota(jnp.int32, (1, ATT_TK), 1)

    def tile_of(ref, lead):
        def get(kt):
            return ref[lead + (pl.ds(pl.multiple_of(kt * ATT_TK, ATT_TK), ATT_TK), slice(None))].astype(BF16)
        return get

    def sel_bias(kt):
        kpos = key_pos(kt)
        expand = jnp.where((kpos // CMP_BLOCK) == blk_col, 1.0, 0.0).astype(BF16)
        in_sel = jnp.dot(sel, expand, preferred_element_type=F32) > 0.5
        return jnp.where(jnp.logical_and(in_sel, kpos <= pos_col), 0.0, NEG_INF)

    scratch = (m_scr, l_scr, acc_scr)
    kt_hi = (q0 + tq + ATT_TK - 1) // ATT_TK
    lead_s = (0, 0, 0, 0)
    o_s = _flash(q_ref, tile_of(ks_ref, lead_s), tile_of(vs_ref, lead_s), 0, kt_hi, sel_bias, *scratch)
    n_w = WINDOW + tq
    w0 = pl.multiple_of(jnp.maximum(q0 - WINDOW, 0), tq)
    dist = pos_col - (w0 + lax.broadcasted_iota(jnp.int32, (1, n_w), 1))
    w_bias = jnp.where(jnp.logical_and(dist >= 0, dist <= WINDOW), 0.0, NEG_INF)
    kw = kw_ref[0, 0, 0, pl.ds(w0, n_w), :].astype(BF16)
    vw = vw_ref[0, 0, 0, pl.ds(w0, n_w), :].astype(BF16)
    s_w = lax.dot_general(q2, kw, NT_DIMS, preferred_element_type=F32).reshape(HPG, tq, n_w) + w_bias[None]
    p_w = jnp.exp(s_w - jnp.max(s_w, axis=-1, keepdims=True))
    l_w = jnp.sum(p_w, axis=-1, keepdims=True)
    o_w = jnp.dot(p_w.reshape(HPG * tq, n_w).astype(BF16), vw, preferred_element_type=F32)
    o_w = o_w.reshape(HPG, tq, HEAD_DIM) / l_w

    gate = jax.nn.sigmoid(gate_ref[0, 0])
    for h in range(HPG):
        o = (gate[:, 3 * h:3 * h + 1] * o_c[h] + gate[:, 3 * h + 1:3 * h + 2] * o_s[h]
             + gate[:, 3 * h + 2:3 * h + 3] * o_w[h])
        o_ref[:, h * HEAD_DIM:(h + 1) * HEAD_DIM] = o


def nsa_prompt_attn(q, kc, vc, rows, win, gates):
    b, _, t, _ = q.shape
    assert t % ATT_TK == 0 and t >= WINDOW + ATT_TQ
    nb = kc.shape[2]
    n_q = t // ATT_TQ
    kv_spec = lambda comp: pl.BlockSpec((1, 1, 1, 1, t, HEAD_DIM), lambda bi, g, qi: (bi, 0, comp, g, 0, 0))
    win_spec = lambda comp: pl.BlockSpec((1, 1, 1, t, HEAD_DIM), lambda bi, g, qi: (bi, comp, g, 0, 0))
    cmp_spec = pl.BlockSpec((1, 1, nb, HEAD_DIM), lambda bi, g, qi: (bi, g, 0, 0))
    return pl.pallas_call(
        _nsa_prompt_attn_kernel,
        grid=(b, N_KV_HEADS, n_q),
        in_specs=[
            pl.BlockSpec((1, HPG, ATT_TQ, HEAD_DIM), lambda bi, g, qi: (bi, g, qi, 0)),
            cmp_spec, cmp_spec,
            kv_spec(2), kv_spec(3),
            win_spec(0), win_spec(1),
            pl.BlockSpec((1, 1, ATT_TQ, 3 * HPG), lambda bi, g, qi: (bi, g, qi, 0)),
        ],
        out_specs=pl.BlockSpec((ATT_TQ, HPG * HEAD_DIM), lambda bi, g, qi: (bi * n_q + qi, g)),
        out_shape=jax.ShapeDtypeStruct((b * t, Q_DIM), F32),
        scratch_shapes=[
            pltpu.VMEM((HPG, ATT_TQ, 1), F32),
            pltpu.VMEM((HPG, ATT_TQ, 1), F32),
            pltpu.VMEM((HPG, ATT_TQ, HEAD_DIM), F32),
        ],
        compiler_params=pltpu.CompilerParams(
            dimension_semantics=("parallel", "parallel", "arbitrary"), vmem_limit_bytes=VMEM_LIMIT_MID),
        name="nsa_prompt_attn",
    )(q, kc, vc, rows, rows, win, win, gates)


PAGE_SIZE = 128
DEC_ROWS = 512
DEC_NB_PAD = 128
NSA_PROJ_PAD = 5632
BLOCKS_PER_PAGE = PAGE_SIZE // CMP_BLOCK
PAGE_ROWS = N_KV_HEADS * BLOCKS_PER_PAGE
CMP_PAGES_PER_STEP = 32


def _paged_compress_kernel(pt_ref, cache_ref, pe_ref, w1_ref, w2_ref, o_ref, buf, sem, *, comp):
    s = pl.program_id(0)
    n_steps = pl.num_programs(0)

    def block_copy(step, p, row):
        slot = step % 2
        page = pt_ref[step * CMP_PAGES_PER_STEP + p]
        return pltpu.make_async_copy(cache_ref.at[page, comp, row], buf.at[slot, :, p * PAGE_ROWS + row, :],
                                     sem.at[slot])

    def for_all_blocks(step, fn):
        for p in range(CMP_PAGES_PER_STEP):
            for row in range(PAGE_ROWS):
                fn(block_copy(step, p, row))

    @pl.when(s == 0)
    def _():
        for_all_blocks(0, lambda cp: cp.start())

    @pl.when(s + 1 < n_steps)
    def _():
        for_all_blocks(s + 1, lambda cp: cp.start())

    for_all_blocks(s, lambda cp: cp.wait())
    slot = s % 2
    n_rows = CMP_PAGES_PER_STEP * PAGE_ROWS
    o_ref[...] = _compress_rows(lambda l: buf[slot, l], n_rows, pe_ref, w1_ref, w2_ref)


def paged_compress(page_table_flat, cache, pe, w1, w2, *, comp):
    n_pages = page_table_flat.shape[0]
    assert n_pages % CMP_PAGES_PER_STEP == 0
    rows = CMP_PAGES_PER_STEP * PAGE_ROWS
    return pl.pallas_call(
        functools.partial(_paged_compress_kernel, comp=comp),
        grid_spec=pltpu.PrefetchScalarGridSpec(
            num_scalar_prefetch=1,
            grid=(n_pages // CMP_PAGES_PER_STEP,),
            in_specs=[pl.BlockSpec(memory_space=pl.ANY)] + _compress_weight_specs(),
            out_specs=pl.BlockSpec((rows, HEAD_DIM), lambda i, pt: (i, 0)),
            scratch_shapes=[
                pltpu.VMEM((2, CMP_BLOCK, rows, HEAD_DIM), F32),
                pltpu.SemaphoreType.DMA((2,)),
            ],
        ),
        out_shape=jax.ShapeDtypeStruct((n_pages * PAGE_ROWS, HEAD_DIM), F32),
        compiler_params=pltpu.CompilerParams(
            dimension_semantics=("arbitrary",), vmem_limit_bytes=VMEM_LIMIT_MID),
        name="paged_compress",
    )(page_table_flat, cache, pe, w1, w2)


def _group_diag(rows_per_group, cols_per_group, shape):
    r = lax.broadcasted_iota(jnp.int32, shape, 0) // rows_per_group
    c = lax.broadcasted_iota(jnp.int32, shape, 1) // cols_per_group
    return r == c


def _decode_attn_a_kernel(q_ref, kc_ref, vc_ref, new_ref, gate_ref, win_ref,
                          ocw_ref, idx_ref, wout_ref, *, nb, nb_pad, pos):
    q = q_ref[0]
    kc = kc_ref[0].reshape(N_KV_HEADS * nb_pad, HEAD_DIM).astype(BF16)
    vc = vc_ref[0].reshape(N_KV_HEADS * nb_pad, HEAD_DIM).astype(BF16)
    gate = jax.nn.sigmoid(gate_ref[0])

    shape = (N_HEADS, N_KV_HEADS * nb_pad)
    blk = lax.broadcasted_iota(jnp.int32, shape, 1) % nb_pad
    complete = jnp.logical_and((blk + 1) * CMP_BLOCK - 1 <= pos, blk < nb)
    c_mask = jnp.logical_and(_group_diag(HPG, nb_pad, shape), complete)
    s_c = lax.dot_general(q, kc, NT_DIMS, preferred_element_type=F32)
    p_c = _softmax_masked(s_c, c_mask, axis=-1)
    o_c = jnp.dot(p_c.astype(BF16), vc, preferred_element_type=F32)

    blk_col = lax.broadcasted_iota(jnp.int32, (nb_pad, 1), 0)
    complete_t = jnp.logical_and((blk_col + 1) * CMP_BLOCK - 1 <= pos, blk_col < nb)
    lane16 = lax.broadcasted_iota(jnp.int32, (nb_pad, N_HEADS), 1)
    lane = lax.broadcasted_iota(jnp.int32, (nb_pad, 128), 1)
    imp = jnp.zeros((nb_pad, 128), F32)
    for g in range(N_KV_HEADS):
        s_t = lax.dot_general(kc[g * nb_pad:(g + 1) * nb_pad], q, NT_DIMS, preferred_element_type=F32)
        p_t = _softmax_masked(s_t, complete_t, axis=0)
        imp_g = jnp.sum(jnp.where(lane16 // HPG == g, p_t, 0.0), axis=1, keepdims=True)
        imp = jnp.where(lane == g, imp_g, imp)
    cur = pos // CMP_BLOCK
    reach = jnp.logical_and(blk_col <= cur, blk_col < nb)
    forced = jnp.logical_and(reach, jnp.logical_or(blk_col == 0, blk_col >= cur - 1))
    score = jnp.where(forced, FORCED_SCORE, jnp.where(reach, imp, NEG_INF))
    score = jnp.where(blk_col < nb, score, 2.0 * NEG_INF)
    rank = _block_ranks(score, nb)
    blk_f = blk_col.astype(F32)
    picked = [jnp.sum(jnp.where(rank == float(k), blk_f, 0.0), axis=0, keepdims=True) for k in range(N_SELECT)]
    idx_ref[0] = jnp.concatenate(picked, axis=0).astype(jnp.int32)

    wb = win_ref.shape[4]
    kw = win_ref[0, 0, 0].reshape(N_KV_HEADS * wb, HEAD_DIM).astype(BF16)
    vw = win_ref[0, 0, 1].reshape(N_KV_HEADS * wb, HEAD_DIM).astype(BF16)
    k_new = new_ref[0, 4]
    v_new = new_ref[0, 5]
    shape_w = (N_HEADS, N_KV_HEADS * wb)
    w_mask = _group_diag(HPG, wb, shape_w)
    s_w = jnp.where(w_mask, lax.dot_general(q, kw, NT_DIMS, preferred_element_type=F32), NEG_INF)
    s_n = jnp.sum(q.astype(F32) * k_new, axis=-1, keepdims=True)
    m = jnp.maximum(jnp.max(s_w, axis=-1, keepdims=True), s_n)
    e_w = jnp.where(w_mask, jnp.exp(s_w - m), 0.0)
    e_n = jnp.exp(s_n - m)
    denom = jnp.sum(e_w, axis=-1, keepdims=True) + e_n
    o_w = (jnp.dot(e_w.astype(BF16), vw, preferred_element_type=F32)
           + e_n * v_new) / denom

    ocw_ref[0] = gate[:, 0:1] * o_c + gate[:, 2:3] * o_w

    for c in range(2):
        for g in range(N_KV_HEADS):
            wout_ref[0, 0, c, g, 0:wb - 1, :] = win_ref[0, 0, c, g, 1:wb, :]
            wout_ref[0, 0, c, g, wb - 1:wb, :] = new_ref[0, 4 + c, g * HPG:g * HPG + 1, :]


def decode_attn_a(q, kc, vc, new_rows, gates, win, *, nb, pos):
    s = q.shape[0]
    nb_pad = kc.shape[2]
    wb = win.shape[4]
    cmp_spec = pl.BlockSpec((1, N_KV_HEADS, nb_pad, HEAD_DIM), lambda i: (i, 0, 0, 0))
    win_spec = pl.BlockSpec((1, 1, 2, N_KV_HEADS, wb, HEAD_DIM), lambda i: (i, 0, 0, 0, 0, 0))
    return pl.pallas_call(
        functools.partial(_decode_attn_a_kernel, nb=nb, nb_pad=nb_pad, pos=pos),
        grid=(s,),
        in_specs=[
            pl.BlockSpec((1, N_HEADS, HEAD_DIM), lambda i: (i, 0, 0)),
            cmp_spec, cmp_spec,
            pl.BlockSpec((1, 6, N_HEADS, HEAD_DIM), lambda i: (i, 0, 0, 0)),
            pl.BlockSpec((1, N_HEADS, 3), lambda i: (i, 0, 0)),
            win_spec,
        ],
        out_specs=[
            pl.BlockSpec((1, N_HEADS, HEAD_DIM), lambda i: (i, 0, 0)),
            pl.BlockSpec((1, N_SELECT, 128), lambda i: (i, 0, 0)),
            win_spec,
        ],
        out_shape=[
            jax.ShapeDtypeStruct((s, N_HEADS, HEAD_DIM), F32),
            jax.ShapeDtypeStruct((s, N_SELECT, 128), jnp.int32),
            jax.ShapeDtypeStruct(win.shape, F32),
        ],
        compiler_params=pltpu.CompilerParams(
            dimension_semantics=("parallel",), vmem_limit_bytes=VMEM_LIMIT_MID),
        name="decode_attn_a",
    )(q, kc, vc, new_rows, gates, win)


SEL_SLOTS = N_KV_HEADS * N_SELECT


def _decode_attn_b_kernel(pt_ref, idx_ref, q_ref, cache_ref, new_ref, gate_ref, ocw_ref, o_ref,
                          kbuf, vbuf, sem, *, pages_per_seq, pos):
    s = pl.program_id(0)
    n_seq = pl.num_programs(0)
    nb_past = pages_per_seq * BLOCKS_PER_PAGE
    new_blk = pos // CMP_BLOCK
    blks = [idx_ref[s * SEL_SLOTS + slot] for slot in range(SEL_SLOTS)]

    def block_copies(seq, slot):
        buf = seq % 2
        blk = jnp.minimum(idx_ref[seq * SEL_SLOTS + slot], nb_past - 1)
        page = pt_ref[seq * pages_per_seq + blk // BLOCKS_PER_PAGE]
        row = (slot // N_SELECT) * BLOCKS_PER_PAGE + blk % BLOCKS_PER_PAGE
        return (pltpu.make_async_copy(cache_ref.at[page, 2, row], kbuf.at[buf, slot], sem.at[buf, 0]),
                pltpu.make_async_copy(cache_ref.at[page, 3, row], vbuf.at[buf, slot], sem.at[buf, 1]))

    def for_all_blocks(seq, fn):
        for slot in range(SEL_SLOTS):
            for cp in block_copies(seq, slot):
                fn(cp)

    @pl.when(s == 0)
    def _():
        for_all_blocks(0, lambda cp: cp.start())

    @pl.when(s + 1 < n_seq)
    def _():
        for_all_blocks(s + 1, lambda cp: cp.start())

    for_all_blocks(s, lambda cp: cp.wait())
    cur = s % 2

    q = q_ref[0]
    n_keys = SEL_SLOTS * CMP_BLOCK
    shape = (N_HEADS, n_keys)
    lane = lax.broadcasted_iota(jnp.int32, (8, SEL_SLOTS), 1)
    valid = jnp.zeros((8, SEL_SLOTS), F32)
    for slot in range(SEL_SLOTS):
        valid = jnp.where(lane == slot, jnp.where(blks[slot] < nb_past, 1.0, 0.0), valid)
    expand = jnp.where(lax.broadcasted_iota(jnp.int32, (SEL_SLOTS, n_keys), 1) // CMP_BLOCK
                       == lax.broadcasted_iota(jnp.int32, (SEL_SLOTS, n_keys), 0), 1.0, 0.0).astype(BF16)
    valid_keys = jnp.dot(valid.astype(BF16), expand, preferred_element_type=F32)[0:1, :] > 0.5
    ok = jnp.logical_and(_group_diag(HPG, N_SELECT * CMP_BLOCK, shape), valid_keys)

    head_group = lax.broadcasted_iota(jnp.int32, (N_HEADS, 1), 0) // HPG
    sel_new = jnp.zeros((N_HEADS, 1), F32)
    for g in range(N_KV_HEADS):
        hit = blks[g * N_SELECT] == new_blk
        for k in range(1, N_SELECT):
            hit = jnp.logical_or(hit, blks[g * N_SELECT + k] == new_blk)
        sel_new = jnp.where(head_group == g, jnp.where(hit, 1.0, 0.0), sel_new)
    sel_new = sel_new > 0.5

    k_sel = kbuf[cur].reshape(n_keys, HEAD_DIM).astype(BF16)
    v_sel = vbuf[cur].reshape(n_keys, HEAD_DIM).astype(BF16)
    k_new = new_ref[0, 2]
    v_new = new_ref[0, 3]
    s_sel = jnp.where(ok, lax.dot_general(q, k_sel, NT_DIMS, preferred_element_type=F32), NEG_INF)
    s_new = jnp.where(sel_new, jnp.sum(q.astype(F32) * k_new, axis=-1, keepdims=True), NEG_INF)
    m = jnp.maximum(jnp.max(s_sel, axis=-1, keepdims=True), s_new)
    e_sel = jnp.where(ok, jnp.exp(s_sel - m), 0.0)
    e_new = jnp.where(sel_new, jnp.exp(s_new - m), 0.0)
    l = jnp.sum(e_sel, axis=-1, keepdims=True) + e_new
    acc = jnp.dot(e_sel.astype(BF16), v_sel, preferred_element_type=F32) + e_new * v_new
    o_s = jnp.where(l > 0.0, acc / jnp.where(l > 0.0, l, 1.0), 0.0)
    gate = jax.nn.sigmoid(gate_ref[0])
    o_ref[0] = ocw_ref[0] + gate[:, 1:2] * o_s


def decode_attn_b(page_table_flat, idx_flat, q, cache, new_rows, gates, ocw, *, pages_per_seq, pos):
    s = q.shape[0]
    assert pos >= pages_per_seq * PAGE_SIZE - 1
    seq_spec = lambda shape: pl.BlockSpec((1,) + shape, lambda i, pt, idx: (i,) + (0,) * len(shape))
    return pl.pallas_call(
        functools.partial(_decode_attn_b_kernel, pages_per_seq=pages_per_seq, pos=pos),
        grid_spec=pltpu.PrefetchScalarGridSpec(
            num_scalar_prefetch=2,
            grid=(s,),
            in_specs=[
                seq_spec((N_HEADS, HEAD_DIM)),
                pl.BlockSpec(memory_space=pl.ANY),
                seq_spec((6, N_HEADS, HEAD_DIM)),
                seq_spec((N_HEADS, 3)),
                seq_spec((N_HEADS, HEAD_DIM)),
            ],
            out_specs=seq_spec((N_HEADS, HEAD_DIM)),
            scratch_shapes=[
                pltpu.VMEM((2, SEL_SLOTS, CMP_BLOCK, HEAD_DIM), F32),
                pltpu.VMEM((2, SEL_SLOTS, CMP_BLOCK, HEAD_DIM), F32),
                pltpu.SemaphoreType.DMA((2, 2)),
            ],
        ),
        out_shape=jax.ShapeDtypeStruct((s, N_HEADS, HEAD_DIM), F32),
        compiler_params=pltpu.CompilerParams(
            dimension_semantics=("arbitrary",), vmem_limit_bytes=VMEM_LIMIT_MID),
        name="decode_attn_b",
    )(page_table_flat, idx_flat, q, cache, new_rows, gates, ocw)


def nsa_decode_group(pn, row0, s, cache_nsa_kv, state_nsa_win, page_table, cmp_k, cmp_v):
    pages_per_seq = page_table.shape[1]
    past = pages_per_seq * PAGE_SIZE
    nb_dec = past // CMP_BLOCK + 1
    proj = Q_DIM + 6 * KV_DIM
    cos_s, sin_s = rope_tables(jnp.full((s,), past, jnp.int32))
    q_s, rows_s, win_s = nsa_layout(pn, cos_s, sin_s, n_seq=1, seq_len=s, row0=row0, tq=s)
    q_s = q_s[0].transpose(1, 0, 2)
    new_rows = jnp.concatenate([rows_s[0, 0], win_s[0]], axis=0).transpose(2, 0, 1, 3)
    new16 = jnp.repeat(new_rows, HPG, axis=2)
    pt_flat = page_table.reshape(-1)
    cache5 = cache_nsa_kv.reshape(cache_nsa_kv.shape[0], 4, PAGE_ROWS, CMP_BLOCK, HEAD_DIM)

    def summaries(comp, pe, w1, w2):
        past_c = paged_compress(pt_flat, cache5, pe, w1, w2, comp=comp)
        past_c = past_c.reshape(s, pages_per_seq, N_KV_HEADS, BLOCKS_PER_PAGE, HEAD_DIM)
        past_c = past_c.transpose(0, 2, 1, 3, 4).reshape(s, N_KV_HEADS, pages_per_seq * BLOCKS_PER_PAGE, HEAD_DIM)
        new_blk = jnp.pad(new_rows[:, comp].reshape(s * N_KV_HEADS, 1, HEAD_DIM), ((0, 0), (0, CMP_BLOCK - 1), (0, 0)))
        new_c = compress_blocks(new_blk, pe, w1, w2, tr=s * N_KV_HEADS, blocks=[0])
        both = jnp.concatenate([past_c, new_c.reshape(s, N_KV_HEADS, 1, HEAD_DIM)], axis=2)
        return jnp.pad(both, ((0, 0), (0, 0), (0, DEC_NB_PAD - nb_dec), (0, 0)))

    kc_s = summaries(0, *cmp_k)
    vc_s = summaries(1, *cmp_v)
    gates_s = pn[row0:row0 + s, proj:proj + 3 * N_HEADS].reshape(s, N_HEADS, 3)
    ocw, idx, win_out = decode_attn_a(q_s, kc_s, vc_s, new16, gates_s, state_nsa_win, nb=nb_dec, pos=past)
    idx_flat = idx[:, :, :N_KV_HEADS].transpose(0, 2, 1).reshape(-1)
    o_s = decode_attn_b(pt_flat, idx_flat, q_s, cache5, new16, gates_s, ocw, pages_per_seq=pages_per_seq, pos=past)
    return o_s.reshape(s, Q_DIM), new_rows[:, None, :4, :, None, :], win_out


def _matmul_residual_kernel(a_ref, a2_ref, w_ref, x_ref, o_ref, *, first_tiles):
    i = pl.program_id(0)

    @pl.when(i < first_tiles)
    def _():
        o_ref[...] = x_ref[...] + jnp.dot(a_ref[...].astype(BF16), w_ref[...], preferred_element_type=F32)

    @pl.when(i >= first_tiles)
    def _():
        o_ref[...] = x_ref[...] + jnp.dot(a2_ref[...].astype(BF16), w_ref[...], preferred_element_type=F32)


def matmul_residual(a, a2, w, x, *, tm):
    n1, k = a.shape
    n2 = a2.shape[0]
    n, d = x.shape
    assert n == n1 + n2 and n1 % tm == 0 and n2 % tm == 0
    first = n1 // tm
    return pl.pallas_call(
        functools.partial(_matmul_residual_kernel, first_tiles=first),
        grid=(n // tm,),
        in_specs=[
            pl.BlockSpec((tm, k), lambda i: (jnp.minimum(i, first - 1), 0)),
            pl.BlockSpec((tm, k), lambda i: (jnp.maximum(i - first, 0), 0)),
            pl.BlockSpec((k, d), lambda i: (0, 0)),
            pl.BlockSpec((tm, d), lambda i: (i, 0)),
        ],
        out_specs=pl.BlockSpec((tm, d), lambda i: (i, 0)),
        out_shape=jax.ShapeDtypeStruct((n, d), F32),
        compiler_params=pltpu.CompilerParams(
            dimension_semantics=("parallel",), vmem_limit_bytes=VMEM_LIMIT_MID),
        name="matmul_residual",
    )(a, a2, w, x)


def _rmsnorm_kernel(x_ref, g_ref, o_ref):
    x = x_ref[...]
    ms = jnp.mean(x * x, axis=-1, keepdims=True)
    o_ref[...] = x * lax.rsqrt(ms + RMS_EPS) * g_ref[...]


def rmsnorm_rows(x, g, *, row0, n_rows, tm):
    d = x.shape[1]
    assert row0 % tm == 0 and n_rows % tm == 0
    blk0 = row0 // tm
    return pl.pallas_call(
        _rmsnorm_kernel,
        grid=(n_rows // tm,),
        in_specs=[
            pl.BlockSpec((tm, d), lambda i: (blk0 + i, 0)),
            pl.BlockSpec((1, d), lambda i: (0, 0)),
        ],
        out_specs=pl.BlockSpec((tm, d), lambda i: (i, 0)),
        out_shape=jax.ShapeDtypeStruct((n_rows, d), F32),
        compiler_params=pltpu.CompilerParams(dimension_semantics=("parallel",)),
        name="rmsnorm_rows",
    )(x, g.reshape(1, d))


def kernel(x_prompt, x_sample, state_conv, cache_nsa_kv, state_nsa_win, page_table, norm_mix, norm_ffn, norm_final, conv_w_in, conv_w, conv_w_out, nsa_w_in, nsa_w_out, nsa_cmp_pe_k, nsa_cmp_w1_k, nsa_cmp_w2_k, nsa_cmp_pe_v, nsa_cmp_w1_v, nsa_cmp_w2_v, peer_w_q, peer_sub_keys, peer_u, peer_v):
    b, t, d = x_prompt.shape
    s = x_sample.shape[0]
    assert x_sample.shape[1] == 1 and d == D_MODEL
    n_prompt = b * t
    n_dec = DEC_ROWS

    u_all = peer_u.astype(BF16)
    vt_all = peer_v.astype(BF16).transpose(0, 2, 1)

    def peer(x, i):
        return peer_layer(
            x, norm_ffn[i], peer_w_q[i].astype(BF16),
            peer_sub_keys[i].reshape(2 * PEER_HEADS, PEER_KEYS, PEER_HALF_DIM).astype(BF16),
            u_all, vt_all, i)

    x = jnp.concatenate([x_prompt.reshape(n_prompt, d), x_sample.reshape(s, d),
                         jnp.zeros((n_dec - s, d), F32)], axis=0)

    p = rms_matmul_wide(x, norm_mix[0], conv_w_in[0].astype(BF16), tm=512, tn=3 * d // 2)
    st = state_conv[:, 0]
    pad_dec = lambda a: jnp.pad(a, ((0, n_dec - s), (0, 0)))
    x, tail, u_dec = conv_core(p, x, pad_dec(st[:, 0]), pad_dec(st[:, 1]), conv_w[0],
                               conv_w_out[0].astype(BF16), n_prompt=n_prompt, seq_len=t)
    tail_p = tail[:n_prompt // CONV_TM].reshape(b, t // CONV_TM, CONV_TAIL, d)
    conv_state_prompt = tail_p[:, -1, CONV_TAIL - 2:][:, None]
    conv_state_sample = jnp.stack([st[:, 1], u_dec[:s]], axis=1)[:, None]
    x = peer(x, 0)

    proj = Q_DIM + 6 * KV_DIM
    n_gate = 3 * N_HEADS
    w_in = jnp.pad(nsa_w_in[0], ((0, 0), (0, NSA_PROJ_PAD - proj - n_gate))).astype(BF16)
    pn = rms_matmul_wide(x, norm_mix[1], w_in, tm=512, tn=NSA_PROJ_PAD // 2)
    gates = pn[:, proj:proj + n_gate]
    w1_shape = (CMP_PASSES, CMP_PAIR * HEAD_DIM, CMP_HIDDEN)
    w1k, w2k = nsa_cmp_w1_k[0].astype(BF16).reshape(w1_shape), nsa_cmp_w2_k[0].astype(BF16)
    w1v, w2v = nsa_cmp_w1_v[0].astype(BF16).reshape(w1_shape), nsa_cmp_w2_v[0].astype(BF16)
    pe_k, pe_v = nsa_cmp_pe_k[0], nsa_cmp_pe_v[0]

    cos_p, sin_p = rope_tables(jnp.arange(t))
    q_p, rows_p, win_p = nsa_layout(pn, cos_p, sin_p, n_seq=b, seq_len=t, row0=0, tq=ATT_TQ)
    nb = t // CMP_BLOCK
    flat = rows_p.reshape(b * 4 * N_KV_HEADS * nb, CMP_BLOCK, HEAD_DIM)
    tr = N_KV_HEADS * nb
    kc_p = compress_blocks(flat, pe_k, w1k, w2k, tr=tr, blocks=[4 * i for i in range(b)])
    vc_p = compress_blocks(flat, pe_v, w1v, w2v, tr=tr, blocks=[4 * i + 1 for i in range(b)])
    gates_p = gates[:n_prompt].reshape(b, t, N_KV_HEADS, 3 * HPG).transpose(0, 2, 1, 3)
    o_p = nsa_prompt_attn(q_p, kc_p.reshape(b, N_KV_HEADS, nb, HEAD_DIM), vc_p.reshape(b, N_KV_HEADS, nb, HEAD_DIM),
                          rows_p, win_p, gates_p)
    wb_p = min(WINDOW, t)
    nsa_win_prompt = win_p[:, None, :, :, t - wb_p:, :]

    o_s, nsa_rows_sample, nsa_win_sample = nsa_decode_group(
        pn, n_prompt, s, cache_nsa_kv, state_nsa_win, page_table, (pe_k, w1k, w2k), (pe_v, w1v, w2v))
    o_s = jnp.pad(o_s, ((0, n_dec - s), (0, 0)))

    x = matmul_residual(o_p, o_s, nsa_w_out[0].astype(BF16), x, tm=256)
    x = peer(x, 1)

    y_prompt = rmsnorm_rows(x, norm_final, row0=0, n_rows=n_prompt, tm=512).reshape(b, t, d)
    y_sample = rmsnorm_rows(x, norm_final, row0=n_prompt, n_rows=s, tm=s).reshape(s, 1, d)
    return (y_prompt, y_sample, conv_state_prompt, conv_state_sample,
            rows_p, nsa_rows_sample, nsa_win_prompt, nsa_win_sample)
```

```python
import functools
import math

import jax
import jax.numpy as jnp
from jax import lax
from jax.experimental import pallas as pl
from jax.experimental.pallas import tpu as pltpu

F32 = jnp.float32
BF16 = jnp.bfloat16

D_MODEL = 2048
RMS_EPS = 1e-6

PEER_HEADS = 8
PEER_KEYS = 128
PEER_TOPK = 16
PEER_HALF_DIM = 128
PEER_EXPERTS = PEER_KEYS * PEER_KEYS

VMEM_LIMIT_BIG = 60 * 1024 * 1024
VMEM_LIMIT_MID = 48 * 1024 * 1024

NT_DIMS = (((1,), (1,)), ((), ()))


def _gelu_tanh(x):
    c = math.sqrt(2.0 / math.pi)
    half = 0.5 * x
    return half + half * jnp.tanh(x * (c + (c * 0.044715) * (x * x)))


def _rms_matmul_kernel(x_ref, g_ref, w_ref, o_ref, *rest, emit_ht):
    if emit_ht:
        ht_ref, h_scr = rest
    else:
        (h_scr,) = rest

    @pl.when(pl.program_id(1) == 0)
    def _():
        x = x_ref[...]
        ms = jnp.mean(x * x, axis=-1, keepdims=True)
        h = x * lax.rsqrt(ms + RMS_EPS) * g_ref[...]
        h_scr[...] = h.astype(BF16)
        if emit_ht:
            ht_ref[...] = h.T.astype(BF16)

    o_ref[...] = jnp.dot(h_scr[...], w_ref[...], preferred_element_type=F32)


def _rms_matmul_wide_kernel(x_ref, g_ref, w_ref, o_ref):
    x = x_ref[...]
    ms = jnp.mean(x * x, axis=-1, keepdims=True)
    h = (x * lax.rsqrt(ms + RMS_EPS) * g_ref[...]).astype(BF16)
    o_ref[...] = jnp.dot(h, w_ref[...], preferred_element_type=F32)


def rms_matmul_wide(x, g, w, *, tm, tn):
    n, d = x.shape
    m = w.shape[1]
    assert n % tm == 0 and m % tn == 0
    return pl.pallas_call(
        _rms_matmul_wide_kernel,
        grid=(m // tn, n // tm),
        in_specs=[
            pl.BlockSpec((tm, d), lambda j, i: (i, 0)),
            pl.BlockSpec((1, d), lambda j, i: (0, 0)),
            pl.BlockSpec((d, tn), lambda j, i: (0, j)),
        ],
        out_specs=pl.BlockSpec((tm, tn), lambda j, i: (i, j)),
        out_shape=jax.ShapeDtypeStruct((n, m), F32),
        compiler_params=pltpu.CompilerParams(
            dimension_semantics=("arbitrary", "parallel"), vmem_limit_bytes=VMEM_LIMIT_BIG),
        name="rms_matmul_wide",
    )(x, g.reshape(1, d), w)


def rms_matmul(x, g, w, *, tm, tn, emit_ht=False):
    n, d = x.shape
    m = w.shape[1]
    assert n % tm == 0 and m % tn == 0
    out_shape = [jax.ShapeDtypeStruct((n, m), F32)]
    out_specs = [pl.BlockSpec((tm, tn), lambda i, j: (i, j))]
    if emit_ht:
        out_shape.append(jax.ShapeDtypeStruct((d, n), BF16))
        out_specs.append(pl.BlockSpec((d, tm), lambda i, j: (0, i)))
    res = pl.pallas_call(
        functools.partial(_rms_matmul_kernel, emit_ht=emit_ht),
        grid=(n // tm, m // tn),
        in_specs=[
            pl.BlockSpec((tm, d), lambda i, j: (i, 0)),
            pl.BlockSpec((1, d), lambda i, j: (0, 0)),
            pl.BlockSpec((d, tn), lambda i, j: (0, j)),
        ],
        out_specs=out_specs,
        out_shape=out_shape,
        scratch_shapes=[pltpu.VMEM((tm, d), BF16)],
        compiler_params=pltpu.CompilerParams(
            dimension_semantics=("parallel", "arbitrary"), vmem_limit_bytes=VMEM_LIMIT_MID),
        name="rms_matmul_ht" if emit_ht else "rms_matmul",
    )(x, g.reshape(1, d), w)
    return res if emit_ht else res[0]


NEG_BIG = -jnp.inf
NOT_RANKED = 99.0


def _top16_rows(s, break_ties):
    rows = lax.broadcasted_iota(jnp.int32, s.shape, 0).astype(F32)
    rank = jnp.full(s.shape, NOT_RANKED, F32)
    vals = []
    for k in range(PEER_TOPK):
        m = jnp.max(s, axis=0, keepdims=True)
        if break_ties:
            idx = jnp.min(jnp.where(s == m, rows, float(PEER_KEYS)), axis=0, keepdims=True)
            hit = rows == idx
        else:
            hit = s == m
        rank = jnp.where(hit, float(k), rank)
        s = jnp.where(hit, NEG_BIG, s)
        vals.append(m)
    return vals, rank


def _pair_counts(a, b, break_ties):
    t = a[0].shape[1]
    b16 = jnp.concatenate(b, axis=0)
    a_hi = jnp.concatenate(a[8:], axis=0)
    sub8 = lax.broadcasted_iota(jnp.int32, (8, t), 0).astype(F32)
    sub16 = lax.broadcasted_iota(jnp.int32, (16, t), 0).astype(F32)
    cands = [a[0] + b16]
    flats = [sub16]
    for k0 in range(1, 8):
        lim = PEER_TOPK // (k0 + 1)
        c = a[k0] + b16[:8]
        if lim < 8:
            c = jnp.where(sub8 < float(lim), c, NEG_BIG)
        cands.append(c)
        flats.append(sub8 + float(k0 * 16))
    cands.append(a_hi + b[0])
    flats.append((sub8 + 8.0) * 16.0)

    cmax = a[0] + b[0]
    counts = [jnp.zeros(c.shape, F32) for c in cands]
    z = jnp.zeros((1, t), F32)
    for _ in range(PEER_TOPK):
        m = cands[0].max(axis=0, keepdims=True)
        for c in cands[1:]:
            m = jnp.maximum(m, c.max(axis=0, keepdims=True))
        if break_ties:
            idx = None
            for c, f in zip(cands, flats):
                i = jnp.min(jnp.where(c == m, f, 999.0), axis=0, keepdims=True)
                idx = i if idx is None else jnp.minimum(idx, i)
            hits = [f == idx for f in flats]
        else:
            hits = [c == m for c in cands]
        for j, hit in enumerate(hits):
            counts[j] = counts[j] + hit.astype(F32)
            cands[j] = jnp.where(hit, NEG_BIG, cands[j])
        z = z + jnp.exp(m - cmax)
    n = [c.sum(axis=0, keepdims=True) for c in counts[:8]]
    n += [counts[8][r:r + 1] for r in range(8)]
    return n, z


def _peer_route_kernel(q_ref, keys_ref, r1_ref, cnt_ref, a_ref, b_ref):
    def head(h, carry):
        col = pl.multiple_of(h * (2 * PEER_HALF_DIM), 2 * PEER_HALF_DIM)
        q0 = q_ref[:, pl.ds(col, PEER_HALF_DIM)].astype(BF16)
        q1 = q_ref[:, pl.ds(col + PEER_HALF_DIM, PEER_HALF_DIM)].astype(BF16)
        s0 = lax.dot_general(keys_ref[2 * h], q0, NT_DIMS, preferred_element_type=F32)
        s1 = lax.dot_general(keys_ref[2 * h + 1], q1, NT_DIMS, preferred_element_type=F32)

        def route(break_ties):
            v0, rank0 = _top16_rows(s0, break_ties)
            v1, rank1 = _top16_rows(s1, break_ties)
            n, z = _pair_counts(v0, v1, break_ties)
            cnt = jnp.zeros(s0.shape, F32)
            for k0 in range(PEER_TOPK):
                cnt = jnp.where(rank0 == float(k0), n[k0], cnt)
            r1_ref[h] = rank1.astype(r1_ref.dtype)
            cnt_ref[h] = cnt
            a_ref[h] = jnp.exp(s0 - v0[0])
            b_ref[h] = (jnp.exp(s1 - v1[0]) / z).astype(b_ref.dtype)
            ranked = (jnp.sum(jnp.where(rank0 < NOT_RANKED, 1.0, 0.0), axis=0, keepdims=True)
                      + jnp.sum(jnp.where(rank1 < NOT_RANKED, 1.0, 0.0), axis=0, keepdims=True))
            pairs = n[0]
            for nk in n[1:]:
                pairs = pairs + nk
            clean = jnp.logical_and(ranked == float(2 * PEER_TOPK), pairs == float(PEER_TOPK))
            return jnp.min(jnp.where(clean, 1.0, 0.0))

        all_clean = route(break_ties=False)

        @pl.when(all_clean < 0.5)
        def _():
            route(break_ties=True)

        return carry

    lax.fori_loop(0, PEER_HEADS, head, 0)


def peer_route(q, keys, *, tm):
    n = q.shape[0]
    assert n % tm == 0
    sds = [jax.ShapeDtypeStruct((PEER_HEADS, PEER_KEYS, n), dt) for dt in (BF16, F32, F32, BF16)]
    ospec = pl.BlockSpec((PEER_HEADS, PEER_KEYS, tm), lambda i: (0, 0, i))
    return pl.pallas_call(
        _peer_route_kernel,
        grid=(n // tm,),
        in_specs=[
            pl.BlockSpec((tm, PEER_HEADS * 2 * PEER_HALF_DIM), lambda i: (i, 0)),
            pl.BlockSpec((2 * PEER_HEADS, PEER_KEYS, PEER_HALF_DIM), lambda i: (0, 0, 0)),
        ],
        out_specs=[ospec] * 4,
        out_shape=sds,
        compiler_params=pltpu.CompilerParams(
            dimension_semantics=("parallel",), vmem_limit_bytes=VMEM_LIMIT_MID),
        name="peer_route",
    )(q, keys)


PEER_TE = 1024
PEER_ROWS = PEER_TE // PEER_KEYS


PEER_CHUNK = 16


def _peer_gate_block(act_ref, p_ref, r1_ref, b_ref, cnt_ref, a_ref, r, tc):
    cols = slice(tc * 128, (tc + 1) * 128)
    n_chunks = PEER_KEYS // PEER_CHUNK
    w = [jnp.zeros((PEER_CHUNK, 128), BF16) for _ in range(n_chunks)]
    zero = jnp.zeros((), BF16)
    for h in range(PEER_HEADS):
        cnt = jnp.broadcast_to(cnt_ref[h, 0, r:r + 1, cols], (PEER_CHUNK, 128)).astype(BF16)
        a = jnp.broadcast_to(a_ref[h, 0, r:r + 1, cols], (PEER_CHUNK, 128)).astype(BF16)
        for c in range(n_chunks):
            rows = slice(c * PEER_CHUNK, (c + 1) * PEER_CHUNK)
            first = jnp.minimum(jnp.maximum(cnt - r1_ref[h, rows, cols], zero), a)
            w[c] = w[c] + first * b_ref[h, rows, cols]
    for c in range(n_chunks):
        rows = slice(r * PEER_KEYS + c * PEER_CHUNK, r * PEER_KEYS + (c + 1) * PEER_CHUNK)
        g = _gelu_tanh(act_ref[rows, cols])
        p_ref[rows, cols] = (w[c].astype(F32) * g).astype(BF16)


def _peer_main_kernel(ht_ref, r1_ref, b_ref, cnt_ref, a_ref, u_ref, vt_ref, x_ref, o_ref,
                      acc_ref, act0_ref, act1_ref, p0_ref, p1_ref, *, tm, n_e):
    s = pl.program_id(1)
    acts = (act0_ref, act1_ref)
    ps = (p0_ref, p1_ref)

    def act_piece(parity, mh, nh):
        rows = slice(mh * (PEER_TE // 2), (mh + 1) * (PEER_TE // 2))
        cols = slice(nh * (tm // 2), (nh + 1) * (tm // 2))
        acts[parity][rows, cols] = jnp.dot(u_ref[rows, :], ht_ref[:, cols], preferred_element_type=F32)

    def out_piece(parity, mh, nh):
        d = acc_ref.shape[0]
        rows = slice(mh * (d // 2), (mh + 1) * (d // 2))
        cols = slice(nh * (tm // 2), (nh + 1) * (tm // 2))
        acc_ref[rows, cols] += jnp.dot(vt_ref[rows, :], ps[parity][:, cols], preferred_element_type=F32)

    def step(parity, do_act, do_gates, do_out):
        if do_act:
            for mh in range(2):
                for nh in range(2):
                    act_piece(parity, mh, nh)
        for nh in range(2):
            if do_gates:
                for tc in range(nh * (tm // 256), (nh + 1) * (tm // 256)):
                    for r in range(PEER_ROWS):
                        _peer_gate_block(acts[1 - parity], ps[0], r1_ref, b_ref, cnt_ref, a_ref, r, tc)
            if do_out:
                for mh in range(2):
                    out_piece(0, mh, nh)

    @pl.when(s == 0)
    def _():
        acc_ref[...] = jnp.zeros_like(acc_ref)
        step(0, True, False, False)

    for parity in range(2):
        @pl.when(jnp.logical_and(jnp.logical_and(s >= 1, s < n_e), s % 2 == parity))
        def _():
            step(parity, True, True, True)

    @pl.when(s == n_e)
    def _():
        step(n_e % 2, False, True, True)
        o_ref[...] = x_ref[...] + acc_ref[...].T


def peer_main(ht, route, u_all, vt_all, layer, x, *, tm):
    d, n = ht.shape
    rank1, cnt0, a, b = route
    n_e = PEER_EXPERTS // PEER_TE
    rows_shape = (PEER_HEADS, n_e, PEER_ROWS, n)
    clamp = lambda v: jnp.clip(v, 0, n_e - 1)
    full_spec = pl.BlockSpec((PEER_HEADS, PEER_KEYS, tm), lambda i, s: (0, 0, i))
    rows_spec = pl.BlockSpec((PEER_HEADS, 1, PEER_ROWS, tm), lambda i, s: (0, clamp(s - 1), 0, i))
    return pl.pallas_call(
        functools.partial(_peer_main_kernel, tm=tm, n_e=n_e),
        grid=(n // tm, n_e + 1),
        in_specs=[
            pl.BlockSpec((d, tm), lambda i, s: (0, i)),
            full_spec, full_spec, rows_spec, rows_spec,
            pl.BlockSpec((None, PEER_TE, d), lambda i, s: (layer, clamp(s), 0)),
            pl.BlockSpec((None, d, PEER_TE), lambda i, s: (layer, 0, clamp(s - 1))),
            pl.BlockSpec((tm, d), lambda i, s: (i, 0)),
        ],
        out_specs=pl.BlockSpec((tm, d), lambda i, s: (i, 0)),
        out_shape=jax.ShapeDtypeStruct((n, d), F32),
        scratch_shapes=[
            pltpu.VMEM((d, tm), F32),
            pltpu.VMEM((PEER_TE, tm), F32),
            pltpu.VMEM((PEER_TE, tm), F32),
            pltpu.VMEM((PEER_TE, tm), BF16),
            pltpu.VMEM((PEER_TE, tm), BF16),
        ],
        compiler_params=pltpu.CompilerParams(
            dimension_semantics=("parallel", "arbitrary"), vmem_limit_bytes=VMEM_LIMIT_BIG),
        name="peer_main",
    )(ht, rank1, b, cnt0.reshape(rows_shape), a.reshape(rows_shape), u_all, vt_all, x)


def peer_layer(x, g, w_q, keys, u_all, vt_all, layer, *, tm_q=512, tm_route=256, tm_main=512):
    q, ht = rms_matmul(x, g, w_q, tm=tm_q, tn=1024, emit_ht=True)
    route = peer_route(q, keys, tm=tm_route)
    return peer_main(ht, route, u_all, vt_all, layer, x, tm=tm_main)


CONV_TM = 256
CONV_TAIL = 8


def _conv_core_kernel(b_ref, c_ref, h_ref, x_ref, s0_ref, s1_ref, cw_ref, wout_ref,
                      o_ref, tail_ref, us_ref, ubuf, *, prompt_tiles, tiles_per_seq):
    i = pl.program_id(0)
    tm = CONV_TM
    u = c_ref[...] * h_ref[...]
    w0 = cw_ref[0:1, :]
    w1 = cw_ref[1:2, :]
    w2 = cw_ref[2:3, :]

    @pl.when(i % tiles_per_seq == 0)
    def _():
        ubuf[0:CONV_TAIL, :] = jnp.zeros((CONV_TAIL, D_MODEL), F32)

    ubuf[CONV_TAIL:CONV_TAIL + tm, :] = u
    tail_ref[0] = u[tm - CONV_TAIL:, :]

    @pl.when(i < prompt_tiles)
    def _():
        z = ubuf[CONV_TAIL - 2:CONV_TAIL - 2 + tm, :] * w0 + ubuf[CONV_TAIL - 1:CONV_TAIL - 1 + tm, :] * w1 + u * w2
        y = jnp.dot((b_ref[...] * z).astype(BF16), wout_ref[...], preferred_element_type=F32)
        o_ref[...] = x_ref[...] + y
        ubuf[0:CONV_TAIL, :] = u[tm - CONV_TAIL:, :]

    @pl.when(i >= prompt_tiles)
    def _():
        z = s0_ref[...] * w0 + s1_ref[...] * w1 + u * w2
        y = jnp.dot((b_ref[...] * z).astype(BF16), wout_ref[...], preferred_element_type=F32)
        o_ref[...] = x_ref[...] + y
        us_ref[...] = u


def conv_core(p, x, s0, s1, cw, wout, *, n_prompt, seq_len):
    n, d = x.shape
    tm = CONV_TM
    n_tiles = n // tm
    prompt_tiles = n_prompt // tm
    tiles_per_seq = seq_len // tm
    n_dec = n - n_prompt
    dec_map = lambda i: (jnp.maximum(i - prompt_tiles, 0), 0)
    return pl.pallas_call(
        functools.partial(_conv_core_kernel, prompt_tiles=prompt_tiles, tiles_per_seq=tiles_per_seq),
        grid=(n_tiles,),
        in_specs=[
            pl.BlockSpec((tm, d), lambda i: (i, 0)),
            pl.BlockSpec((tm, d), lambda i: (i, 1)),
            pl.BlockSpec((tm, d), lambda i: (i, 2)),
            pl.BlockSpec((tm, d), lambda i: (i, 0)),
            pl.BlockSpec((tm, d), dec_map),
            pl.BlockSpec((tm, d), dec_map),
            pl.BlockSpec((3, d), lambda i: (0, 0)),
            pl.BlockSpec((d, d), lambda i: (0, 0)),
        ],
        out_specs=[
            pl.BlockSpec((tm, d), lambda i: (i, 0)),
            pl.BlockSpec((1, CONV_TAIL, d), lambda i: (i, 0, 0)),
            pl.BlockSpec((tm, d), dec_map),
        ],
        out_shape=[
            jax.ShapeDtypeStruct((n, d), F32),
            jax.ShapeDtypeStruct((n_tiles, CONV_TAIL, d), F32),
            jax.ShapeDtypeStruct((n_dec, d), F32),
        ],
        scratch_shapes=[pltpu.VMEM((CONV_TAIL + tm, d), F32)],
        compiler_params=pltpu.CompilerParams(
            dimension_semantics=("arbitrary",), vmem_limit_bytes=VMEM_LIMIT_BIG),
        name="conv_core",
    )(p, p, p, x, s0, s1, cw, wout)


N_HEADS = 16
N_KV_HEADS = 4
HPG = N_HEADS // N_KV_HEADS
HEAD_DIM = 128
Q_DIM = N_HEADS * HEAD_DIM
KV_DIM = N_KV_HEADS * HEAD_DIM
CMP_BLOCK = 64
N_SELECT = 16
WINDOW = 512
ROPE_THETA = 10000.0
NEG_INF = -1e30
FORCED_SCORE = 1e6


def _rope(x, cos, sin_signed):
    return x * cos + pltpu.roll(x, HEAD_DIM // 2, 1) * sin_signed


def _nsa_layout_kernel(p_ref, cos_ref, sin_ref, q_ref, rows_ref, win_ref):
    cos = cos_ref[...]
    sin = sin_ref[...]
    scale = HEAD_DIM ** -0.5
    for h in range(N_HEADS):
        q_ref[0, h] = (_rope(p_ref[:, h * HEAD_DIM:(h + 1) * HEAD_DIM], cos, sin) * scale).astype(BF16)
    for c in range(6):
        for g in range(N_KV_HEADS):
            col = Q_DIM + c * KV_DIM + g * HEAD_DIM
            v = p_ref[:, col:col + HEAD_DIM]
            if c % 2 == 0:
                v = _rope(v, cos, sin)
            if c < 4:
                rows_ref[0, 0, c, g] = v
            else:
                win_ref[0, c - 4, g] = v


def nsa_layout(p, cos, sin, *, n_seq, seq_len, row0, tq):
    width = Q_DIM + 6 * KV_DIM
    n_t = seq_len // tq
    blk0 = row0 // tq
    assert row0 % tq == 0
    return pl.pallas_call(
        _nsa_layout_kernel,
        grid=(n_seq, n_t),
        in_specs=[
            pl.BlockSpec((tq, width), lambda b, t: (blk0 + b * n_t + t, 0)),
            pl.BlockSpec((tq, HEAD_DIM), lambda b, t: (t, 0)),
            pl.BlockSpec((tq, HEAD_DIM), lambda b, t: (t, 0)),
        ],
        out_specs=[
            pl.BlockSpec((1, N_HEADS, tq, HEAD_DIM), lambda b, t: (b, 0, t, 0)),
            pl.BlockSpec((1, 1, 4, N_KV_HEADS, tq, HEAD_DIM), lambda b, t: (b, 0, 0, 0, t, 0)),
            pl.BlockSpec((1, 2, N_KV_HEADS, tq, HEAD_DIM), lambda b, t: (b, 0, 0, t, 0)),
        ],
        out_shape=[
            jax.ShapeDtypeStruct((n_seq, N_HEADS, seq_len, HEAD_DIM), BF16),
            jax.ShapeDtypeStruct((n_seq, 1, 4, N_KV_HEADS, seq_len, HEAD_DIM), F32),
            jax.ShapeDtypeStruct((n_seq, 2, N_KV_HEADS, seq_len, HEAD_DIM), F32),
        ],
        compiler_params=pltpu.CompilerParams(
            dimension_semantics=("parallel", "parallel"), vmem_limit_bytes=VMEM_LIMIT_MID),
        name="nsa_layout",
    )(p, cos, sin)


def rope_tables(pos):
    half = HEAD_DIM // 2
    inv = ROPE_THETA ** (-jnp.arange(half, dtype=F32) / half)
    ang = pos.astype(F32)[:, None] * inv[None, :]
    cos = jnp.cos(ang)
    sin = jnp.sin(ang)
    return jnp.concatenate([cos, cos], axis=1), jnp.concatenate([-sin, sin], axis=1)


CMP_HIDDEN = 256
CMP_PAIR = 2
CMP_PASSES = CMP_BLOCK // CMP_PAIR


def _compress_rows(position_rows, n_rows, pe_ref, w1_ref, w2_ref):
    acc = jnp.zeros((n_rows, CMP_HIDDEN), F32)
    for p in range(CMP_PASSES):
        pair = [position_rows(CMP_PAIR * p + j) + pe_ref[CMP_PAIR * p + j:CMP_PAIR * p + j + 1, :]
                for j in range(CMP_PAIR)]
        kb = jnp.concatenate(pair, axis=1).astype(BF16)
        acc = acc + jnp.dot(kb, w1_ref[p], preferred_element_type=F32)
    hid = _gelu_tanh(acc)
    return jnp.dot(hid.astype(BF16), w2_ref[...], preferred_element_type=F32)


def _compress_kernel(blk_ref, x_ref, pe_ref, w1_ref, w2_ref, o_ref):
    del blk_ref
    o_ref[...] = _compress_rows(lambda l: x_ref[:, l, :], x_ref.shape[0], pe_ref, w1_ref, w2_ref)


def _compress_weight_specs():
    const = lambda n: (lambda *args: (0,) * n)
    return [
        pl.BlockSpec((CMP_BLOCK, HEAD_DIM), const(2)),
        pl.BlockSpec((CMP_PASSES, CMP_PAIR * HEAD_DIM, CMP_HIDDEN), const(3)),
        pl.BlockSpec((CMP_HIDDEN, HEAD_DIM), const(2)),
    ]


def compress_blocks(x, pe, w1, w2, *, tr, blocks):
    blocks = jnp.asarray(blocks, jnp.int32)
    nb = blocks.shape[0]
    return pl.pallas_call(
        _compress_kernel,
        grid_spec=pltpu.PrefetchScalarGridSpec(
            num_scalar_prefetch=1,
            grid=(nb,),
            in_specs=[pl.BlockSpec((tr, CMP_BLOCK, HEAD_DIM), lambda i, blk: (blk[i], 0, 0))]
            + _compress_weight_specs(),
            out_specs=pl.BlockSpec((tr, HEAD_DIM), lambda i, blk: (i, 0)),
        ),
        out_shape=jax.ShapeDtypeStruct((nb * tr, HEAD_DIM), F32),
        compiler_params=pltpu.CompilerParams(
            dimension_semantics=("arbitrary",), vmem_limit_bytes=VMEM_LIMIT_MID),
        name="compress_blocks",
    )(blocks, x, pe, w1, w2)


ATT_TQ = 256
ATT_TK = 512


def _softmax_masked(s, mask, axis):
    sm = jnp.where(mask, s, NEG_INF)
    e = jnp.exp(sm - jnp.max(sm, axis=axis, keepdims=True))
    p = e / jnp.sum(e, axis=axis, keepdims=True)
    return jnp.where(mask, p, 0.0)


def _block_ranks(score_t, n_real):
    blk = lax.broadcasted_iota(jnp.int32, score_t.shape, 0)
    rank = jnp.zeros(score_t.shape, F32)
    for m in range(n_real):
        row = score_t[m:m + 1, :]
        earlier = jnp.where(blk > m, 1.0, 0.0)
        rank = rank + jnp.where(row > score_t, 1.0, jnp.where(row == score_t, earlier, 0.0))
    return rank


def _select_blocks(score_t):
    return jnp.where(_block_ranks(score_t, score_t.shape[0]) < float(N_SELECT), 1.0, 0.0)


def _flash(q_ref, k_tile, v_tile, kt_lo, kt_hi, bias_of, m_scr, l_scr, acc_scr):
    m_scr[...] = jnp.full(m_scr.shape, NEG_INF, F32)
    l_scr[...] = jnp.zeros(l_scr.shape, F32)
    acc_scr[...] = jnp.zeros(acc_scr.shape, F32)
    q2 = q_ref[0].reshape(HPG * ATT_TQ, HEAD_DIM)

    def body(kt, carry):
        s = lax.dot_general(q2, k_tile(kt), NT_DIMS, preferred_element_type=F32).reshape(HPG, ATT_TQ, ATT_TK)
        s = s + bias_of(kt)[None]
        m_old = m_scr[...]
        m_new = jnp.maximum(m_old, jnp.max(s, axis=-1, keepdims=True))
        alpha = jnp.exp(m_old - m_new)
        p = jnp.exp(s - m_new)
        l_scr[...] = alpha * l_scr[...] + jnp.sum(p, axis=-1, keepdims=True)
        pv = jnp.dot(p.reshape(HPG * ATT_TQ, ATT_TK).astype(BF16), v_tile(kt), preferred_element_type=F32)
        acc_scr[...] = alpha * acc_scr[...] + pv.reshape(HPG, ATT_TQ, HEAD_DIM)
        m_scr[...] = m_new
        return carry

    lax.fori_loop(kt_lo, kt_hi, body, 0)
    return acc_scr[...] / l_scr[...]


def _nsa_prompt_attn_kernel(q_ref, kc_ref, vc_ref, ks_ref, vs_ref, kw_ref, vw_ref, gate_ref, o_ref,
                            m_scr, l_scr, acc_scr):
    tq = ATT_TQ
    qi = pl.program_id(2)
    q0 = qi * tq
    q3 = q_ref[0]
    q2 = q3.reshape(HPG * tq, HEAD_DIM)
    nb = kc_ref.shape[2]
    kc = kc_ref[0, 0].astype(BF16)
    vc = vc_ref[0, 0].astype(BF16)

    pos_col = q0 + lax.broadcasted_iota(jnp.int32, (tq, 1), 0)
    blk_row = lax.broadcasted_iota(jnp.int32, (1, nb), 1)
    c_mask = ((blk_row + 1) * CMP_BLOCK - 1) <= pos_col
    s_c = lax.dot_general(q2, kc, NT_DIMS, preferred_element_type=F32).reshape(HPG, tq, nb)
    p_c = _softmax_masked(s_c, c_mask[None], axis=-1)
    o_c = jnp.dot(p_c.reshape(HPG * tq, nb).astype(BF16), vc, preferred_element_type=F32).reshape(HPG, tq, HEAD_DIM)

    pos_row = q0 + lax.broadcasted_iota(jnp.int32, (1, tq), 1)
    blk_col = lax.broadcasted_iota(jnp.int32, (nb, 1), 0)
    c_mask_t = ((blk_col + 1) * CMP_BLOCK - 1) <= pos_row
    imp = jnp.zeros((nb, tq), F32)
    for h in range(HPG):
        s_t = lax.dot_general(kc, q3[h], NT_DIMS, preferred_element_type=F32)
        imp = imp + _softmax_masked(s_t, c_mask_t, axis=0)
    cur = pos_row // CMP_BLOCK
    reach = blk_col <= cur
    forced = jnp.logical_and(reach, jnp.logical_or(blk_col == 0, blk_col >= cur - 1))
    score = jnp.where(forced, FORCED_SCORE, jnp.where(reach, imp, NEG_INF))
    sel_t = _select_blocks(score).astype(BF16)
    eye = jnp.where(lax.broadcasted_iota(jnp.int32, (tq, tq), 0) == lax.broadcasted_iota(jnp.int32, (tq, tq), 1),
                    1.0, 0.0).astype(BF16)
    sel = lax.dot_general(eye, sel_t, NT_DIMS, preferred_element_type=F32).astype(BF16)

    def key_pos(kt):
        return kt * ATT_TK + lax.broadcasted_iota(jnp.int32, (1, ATT_TK), 1)

    def tile_of(ref, lead):
        def get(kt):
            return ref[lead + (pl.ds(pl.multiple_of(kt * ATT_TK, ATT_TK), ATT_TK), slice(None))].astype(BF16)
        return get

    def sel_bias(kt):
        kpos = key_pos(kt)
        expand = jnp.where((kpos // CMP_BLOCK) == blk_col, 1.0, 0.0).astype(BF16)
        in_sel = jnp.dot(sel, expand, preferred_element_type=F32) > 0.5
        return jnp.where(jnp.logical_and(in_sel, kpos <= pos_col), 0.0, NEG_INF)

    scratch = (m_scr, l_scr, acc_scr)
    kt_hi = (q0 + tq + ATT_TK - 1) // ATT_TK
    lead_s = (0, 0, 0, 0)
    o_s = _flash(q_ref, tile_of(ks_ref, lead_s), tile_of(vs_ref, lead_s), 0, kt_hi, sel_bias, *scratch)
    n_w = WINDOW + tq
    w0 = pl.multiple_of(jnp.maximum(q0 - WINDOW, 0), tq)
    dist = pos_col - (w0 + lax.broadcasted_iota(jnp.int32, (1, n_w), 1))
    w_bias = jnp.where(jnp.logical_and(dist >= 0, dist <= WINDOW), 0.0, NEG_INF)
    kw = kw_ref[0, 0, 0, pl.ds(w0, n_w), :].astype(BF16)
    vw = vw_ref[0, 0, 0, pl.ds(w0, n_w), :].astype(BF16)
    s_w = lax.dot_general(q2, kw, NT_DIMS, preferred_element_type=F32).reshape(HPG, tq, n_w) + w_bias[None]
    p_w = jnp.exp(s_w - jnp.max(s_w, axis=-1, keepdims=True))
    l_w = jnp.sum(p_w, axis=-1, keepdims=True)
    o_w = jnp.dot(p_w.reshape(HPG * tq, n_w).astype(BF16), vw, preferred_element_type=F32)
    o_w = o_w.reshape(HPG, tq, HEAD_DIM) / l_w

    gate = jax.nn.sigmoid(gate_ref[0, 0])
    for h in range(HPG):
        o = (gate[:, 3 * h:3 * h + 1] * o_c[h] + gate[:, 3 * h + 1:3 * h + 2] * o_s[h]
             + gate[:, 3 * h + 2:3 * h + 3] * o_w[h])
        o_ref[:, h * HEAD_DIM:(h + 1) * HEAD_DIM] = o


def nsa_prompt_attn(q, kc, vc, rows, win, gates):
    b, _, t, _ = q.shape
    assert t % ATT_TK == 0 and t >= WINDOW + ATT_TQ
    nb = kc.shape[2]
    n_q = t // ATT_TQ
    kv_spec = lambda comp: pl.BlockSpec((1, 1, 1, 1, t, HEAD_DIM), lambda bi, g, qi: (bi, 0, comp, g, 0, 0))
    win_spec = lambda comp: pl.BlockSpec((1, 1, 1, t, HEAD_DIM), lambda bi, g, qi: (bi, comp, g, 0, 0))
    cmp_spec = pl.BlockSpec((1, 1, nb, HEAD_DIM), lambda bi, g, qi: (bi, g, 0, 0))
    return pl.pallas_call(
        _nsa_prompt_attn_kernel,
        grid=(b, N_KV_HEADS, n_q),
        in_specs=[
            pl.BlockSpec((1, HPG, ATT_TQ, HEAD_DIM), lambda bi, g, qi: (bi, g, qi, 0)),
            cmp_spec, cmp_spec,
            kv_spec(2), kv_spec(3),
            win_spec(0), win_spec(1),
            pl.BlockSpec((1, 1, ATT_TQ, 3 * HPG), lambda bi, g, qi: (bi, g, qi, 0)),
        ],
        out_specs=pl.BlockSpec((ATT_TQ, HPG * HEAD_DIM), lambda bi, g, qi: (bi * n_q + qi, g)),
        out_shape=jax.ShapeDtypeStruct((b * t, Q_DIM), F32),
        scratch_shapes=[
            pltpu.VMEM((HPG, ATT_TQ, 1), F32),
            pltpu.VMEM((HPG, ATT_TQ, 1), F32),
            pltpu.VMEM((HPG, ATT_TQ, HEAD_DIM), F32),
        ],
        compiler_params=pltpu.CompilerParams(
            dimension_semantics=("parallel", "parallel", "arbitrary"), vmem_limit_bytes=VMEM_LIMIT_MID),
        name="nsa_prompt_attn",
    )(q, kc, vc, rows, rows, win, win, gates)


PAGE_SIZE = 128
DEC_ROWS = 512
DEC_NB_PAD = 128
NSA_PROJ_PAD = 5632
BLOCKS_PER_PAGE = PAGE_SIZE // CMP_BLOCK
PAGE_ROWS = N_KV_HEADS * BLOCKS_PER_PAGE
CMP_PAGES_PER_STEP = 32


def _paged_compress_kernel(pt_ref, cache_ref, pe_ref, w1_ref, w2_ref, o_ref, buf, sem, *, comp):
    s = pl.program_id(0)
    n_steps = pl.num_programs(0)

    def block_copy(step, p, row):
        slot = step % 2
        page = pt_ref[step * CMP_PAGES_PER_STEP + p]
        return pltpu.make_async_copy(cache_ref.at[page, comp, row], buf.at[slot, :, p * PAGE_ROWS + row, :],
                                     sem.at[slot])

    def for_all_blocks(step, fn):
        for p in range(CMP_PAGES_PER_STEP):
            for row in range(PAGE_ROWS):
                fn(block_copy(step, p, row))

    @pl.when(s == 0)
    def _():
        for_all_blocks(0, lambda cp: cp.start())

    @pl.when(s + 1 < n_steps)
    def _():
        for_all_blocks(s + 1, lambda cp: cp.start())

    for_all_blocks(s, lambda cp: cp.wait())
    slot = s % 2
    n_rows = CMP_PAGES_PER_STEP * PAGE_ROWS
    o_ref[...] = _compress_rows(lambda l: buf[slot, l], n_rows, pe_ref, w1_ref, w2_ref)


def paged_compress(page_table_flat, cache, pe, w1, w2, *, comp):
    n_pages = page_table_flat.shape[0]
    assert n_pages % CMP_PAGES_PER_STEP == 0
    rows = CMP_PAGES_PER_STEP * PAGE_ROWS
    return pl.pallas_call(
        functools.partial(_paged_compress_kernel, comp=comp),
        grid_spec=pltpu.PrefetchScalarGridSpec(
            num_scalar_prefetch=1,
            grid=(n_pages // CMP_PAGES_PER_STEP,),
            in_specs=[pl.BlockSpec(memory_space=pl.ANY)] + _compress_weight_specs(),
            out_specs=pl.BlockSpec((rows, HEAD_DIM), lambda i, pt: (i, 0)),
            scratch_shapes=[
                pltpu.VMEM((2, CMP_BLOCK, rows, HEAD_DIM), F32),
                pltpu.SemaphoreType.DMA((2,)),
            ],
        ),
        out_shape=jax.ShapeDtypeStruct((n_pages * PAGE_ROWS, HEAD_DIM), F32),
        compiler_params=pltpu.CompilerParams(
            dimension_semantics=("arbitrary",), vmem_limit_bytes=VMEM_LIMIT_MID),
        name="paged_compress",
    )(page_table_flat, cache, pe, w1, w2)


def _group_diag(rows_per_group, cols_per_group, shape):
    r = lax.broadcasted_iota(jnp.int32, shape, 0) // rows_per_group
    c = lax.broadcasted_iota(jnp.int32, shape, 1) // cols_per_group
    return r == c


def _decode_attn_a_kernel(q_ref, kc_ref, vc_ref, new_ref, gate_ref, win_ref,
                          ocw_ref, idx_ref, wout_ref, *, nb, nb_pad, pos):
    q = q_ref[0]
    kc = kc_ref[0].reshape(N_KV_HEADS * nb_pad, HEAD_DIM).astype(BF16)
    vc = vc_ref[0].reshape(N_KV_HEADS * nb_pad, HEAD_DIM).astype(BF16)
    gate = jax.nn.sigmoid(gate_ref[0])

    shape = (N_HEADS, N_KV_HEADS * nb_pad)
    blk = lax.broadcasted_iota(jnp.int32, shape, 1) % nb_pad
    complete = jnp.logical_and((blk + 1) * CMP_BLOCK - 1 <= pos, blk < nb)
    c_mask = jnp.logical_and(_group_diag(HPG, nb_pad, shape), complete)
    s_c = lax.dot_general(q, kc, NT_DIMS, preferred_element_type=F32)
    p_c = _softmax_masked(s_c, c_mask, axis=-1)
    o_c = jnp.dot(p_c.astype(BF16), vc, preferred_element_type=F32)

    blk_col = lax.broadcasted_iota(jnp.int32, (nb_pad, 1), 0)
    complete_t = jnp.logical_and((blk_col + 1) * CMP_BLOCK - 1 <= pos, blk_col < nb)
    lane16 = lax.broadcasted_iota(jnp.int32, (nb_pad, N_HEADS), 1)
    lane = lax.broadcasted_iota(jnp.int32, (nb_pad, 128), 1)
    imp = jnp.zeros((nb_pad, 128), F32)
    for g in range(N_KV_HEADS):
        s_t = lax.dot_general(kc[g * nb_pad:(g + 1) * nb_pad], q, NT_DIMS, preferred_element_type=F32)
        p_t = _softmax_masked(s_t, complete_t, axis=0)
        imp_g = jnp.sum(jnp.where(lane16 // HPG == g, p_t, 0.0), axis=1, keepdims=True)
        imp = jnp.where(lane == g, imp_g, imp)
    cur = pos // CMP_BLOCK
    reach = jnp.logical_and(blk_col <= cur, blk_col < nb)
    forced = jnp.logical_and(reach, jnp.logical_or(blk_col == 0, blk_col >= cur - 1))
    score = jnp.where(forced, FORCED_SCORE, jnp.where(reach, imp, NEG_INF))
    score = jnp.where(blk_col < nb, score, 2.0 * NEG_INF)
    rank = _block_ranks(score, nb)
    blk_f = blk_col.astype(F32)
    picked = [jnp.sum(jnp.where(rank == float(k), blk_f, 0.0), axis=0, keepdims=True) for k in range(N_SELECT)]
    idx_ref[0] = jnp.concatenate(picked, axis=0).astype(jnp.int32)

    wb = win_ref.shape[4]
    kw = win_ref[0, 0, 0].reshape(N_KV_HEADS * wb, HEAD_DIM).astype(BF16)
    vw = win_ref[0, 0, 1].reshape(N_KV_HEADS * wb, HEAD_DIM).astype(BF16)
    k_new = new_ref[0, 4]
    v_new = new_ref[0, 5]
    shape_w = (N_HEADS, N_KV_HEADS * wb)
    w_mask = _group_diag(HPG, wb, shape_w)
    s_w = jnp.where(w_mask, lax.dot_general(q, kw, NT_DIMS, preferred_element_type=F32), NEG_INF)
    s_n = jnp.sum(q.astype(F32) * k_new, axis=-1, keepdims=True)
    m = jnp.maximum(jnp.max(s_w, axis=-1, keepdims=True), s_n)
    e_w = jnp.where(w_mask, jnp.exp(s_w - m), 0.0)
    e_n = jnp.exp(s_n - m)
    denom = jnp.sum(e_w, axis=-1, keepdims=True) + e_n
    o_w = (jnp.dot(e_w.astype(BF16), vw, preferred_element_type=F32)
           + e_n * v_new) / denom

    ocw_ref[0] = gate[:, 0:1] * o_c + gate[:, 2:3] * o_w

    for c in range(2):
        for g in range(N_KV_HEADS):
            wout_ref[0, 0, c, g, 0:wb - 1, :] = win_ref[0, 0, c, g, 1:wb, :]
            wout_ref[0, 0, c, g, wb - 1:wb, :] = new_ref[0, 4 + c, g * HPG:g * HPG + 1, :]


def decode_attn_a(q, kc, vc, new_rows, gates, win, *, nb, pos):
    s = q.shape[0]
    nb_pad = kc.shape[2]
    wb = win.shape[4]
    cmp_spec = pl.BlockSpec((1, N_KV_HEADS, nb_pad, HEAD_DIM), lambda i: (i, 0, 0, 0))
    win_spec = pl.BlockSpec((1, 1, 2, N_KV_HEADS, wb, HEAD_DIM), lambda i: (i, 0, 0, 0, 0, 0))
    return pl.pallas_call(
        functools.partial(_decode_attn_a_kernel, nb=nb, nb_pad=nb_pad, pos=pos),
        grid=(s,),
        in_specs=[
            pl.BlockSpec((1, N_HEADS, HEAD_DIM), lambda i: (i, 0, 0)),
            cmp_spec, cmp_spec,
            pl.BlockSpec((1, 6, N_HEADS, HEAD_DIM), lambda i: (i, 0, 0, 0)),
            pl.BlockSpec((1, N_HEADS, 3), lambda i: (i, 0, 0)),
            win_spec,
        ],
        out_specs=[
            pl.BlockSpec((1, N_HEADS, HEAD_DIM), lambda i: (i, 0, 0)),
            pl.BlockSpec((1, N_SELECT, 128), lambda i: (i, 0, 0)),
            win_spec,
        ],
        out_shape=[
            jax.ShapeDtypeStruct((s, N_HEADS, HEAD_DIM), F32),
            jax.ShapeDtypeStruct((s, N_SELECT, 128), jnp.int32),
            jax.ShapeDtypeStruct(win.shape, F32),
        ],
        compiler_params=pltpu.CompilerParams(
            dimension_semantics=("parallel",), vmem_limit_bytes=VMEM_LIMIT_MID),
        name="decode_attn_a",
    )(q, kc, vc, new_rows, gates, win)


SEL_SLOTS = N_KV_HEADS * N_SELECT


def _decode_attn_b_kernel(pt_ref, idx_ref, q_ref, cache_ref, new_ref, gate_ref, ocw_ref, o_ref,
                          kbuf, vbuf, sem, *, pages_per_seq, pos):
    s = pl.program_id(0)
    n_seq = pl.num_programs(0)
    nb_past = pages_per_seq * BLOCKS_PER_PAGE
    new_blk = pos // CMP_BLOCK
    blks = [idx_ref[s * SEL_SLOTS + slot] for slot in range(SEL_SLOTS)]

    def block_copies(seq, slot):
        buf = seq % 2
        blk = jnp.minimum(idx_ref[seq * SEL_SLOTS + slot], nb_past - 1)
        page = pt_ref[seq * pages_per_seq + blk // BLOCKS_PER_PAGE]
        row = (slot // N_SELECT) * BLOCKS_PER_PAGE + blk % BLOCKS_PER_PAGE
        return (pltpu.make_async_copy(cache_ref.at[page, 2, row], kbuf.at[buf, slot], sem.at[buf, 0]),
                pltpu.make_async_copy(cache_ref.at[page, 3, row], vbuf.at[buf, slot], sem.at[buf, 1]))

    def for_all_blocks(seq, fn):
        for slot in range(SEL_SLOTS):
            for cp in block_copies(seq, slot):
                fn(cp)

    @pl.when(s == 0)
    def _():
        for_all_blocks(0, lambda cp: cp.start())

    @pl.when(s + 1 < n_seq)
    def _():
        for_all_blocks(s + 1, lambda cp: cp.start())

    for_all_blocks(s, lambda cp: cp.wait())
    cur = s % 2

    q = q_ref[0]
    n_keys = SEL_SLOTS * CMP_BLOCK
    shape = (N_HEADS, n_keys)
    lane = lax.broadcasted_iota(jnp.int32, (8, SEL_SLOTS), 1)
    valid = jnp.zeros((8, SEL_SLOTS), F32)
    for slot in range(SEL_SLOTS):
        valid = jnp.where(lane == slot, jnp.where(blks[slot] < nb_past, 1.0, 0.0), valid)
    expand = jnp.where(lax.broadcasted_iota(jnp.int32, (SEL_SLOTS, n_keys), 1) // CMP_BLOCK
                       == lax.broadcasted_iota(jnp.int32, (SEL_SLOTS, n_keys), 0), 1.0, 0.0).astype(BF16)
    valid_keys = jnp.dot(valid.astype(BF16), expand, preferred_element_type=F32)[0:1, :] > 0.5
    ok = jnp.logical_and(_group_diag(HPG, N_SELECT * CMP_BLOCK, shape), valid_keys)

    head_group = lax.broadcasted_iota(jnp.int32, (N_HEADS, 1), 0) // HPG
    sel_new = jnp.zeros((N_HEADS, 1), F32)
    for g in range(N_KV_HEADS):
        hit = blks[g * N_SELECT] == new_blk
        for k in range(1, N_SELECT):
            hit = jnp.logical_or(hit, blks[g * N_SELECT + k] == new_blk)
        sel_new = jnp.where(head_group == g, jnp.where(hit, 1.0, 0.0), sel_new)
    sel_new = sel_new > 0.5

    k_sel = kbuf[cur].reshape(n_keys, HEAD_DIM).astype(BF16)
    v_sel = vbuf[cur].reshape(n_keys, HEAD_DIM).astype(BF16)
    k_new = new_ref[0, 2]
    v_new = new_ref[0, 3]
    s_sel = jnp.where(ok, lax.dot_general(q, k_sel, NT_DIMS, preferred_element_type=F32), NEG_INF)
    s_new = jnp.where(sel_new, jnp.sum(q.astype(F32) * k_new, axis=-1, keepdims=True), NEG_INF)
    m = jnp.maximum(jnp.max(s_sel, axis=-1, keepdims=True), s_new)
    e_sel = jnp.where(ok, jnp.exp(s_sel - m), 0.0)
    e_new = jnp.where(sel_new, jnp.exp(s_new - m), 0.0)
    l = jnp.sum(e_sel, axis=-1, keepdims=True) + e_new
    acc = jnp.dot(e_sel.astype(BF16), v_sel, preferred_element_type=F32) + e_new * v_new
    o_s = jnp.where(l > 0.0, acc / jnp.where(l > 0.0, l, 1.0), 0.0)
    gate = jax.nn.sigmoid(gate_ref[0])
    o_ref[0] = ocw_ref[0] + gate[:, 1:2] * o_s


def decode_attn_b(page_table_flat, idx_flat, q, cache, new_rows, gates, ocw, *, pages_per_seq, pos):
    s = q.shape[0]
    assert pos >= pages_per_seq * PAGE_SIZE - 1
    seq_spec = lambda shape: pl.BlockSpec((1,) + shape, lambda i, pt, idx: (i,) + (0,) * len(shape))
    return pl.pallas_call(
        functools.partial(_decode_attn_b_kernel, pages_per_seq=pages_per_seq, pos=pos),
        grid_spec=pltpu.PrefetchScalarGridSpec(
            num_scalar_prefetch=2,
            grid=(s,),
            in_specs=[
                seq_spec((N_HEADS, HEAD_DIM)),
                pl.BlockSpec(memory_space=pl.ANY),
                seq_spec((6, N_HEADS, HEAD_DIM)),
                seq_spec((N_HEADS, 3)),
                seq_spec((N_HEADS, HEAD_DIM)),
            ],
            out_specs=seq_spec((N_HEADS, HEAD_DIM)),
            scratch_shapes=[
                pltpu.VMEM((2, SEL_SLOTS, CMP_BLOCK, HEAD_DIM), F32),
                pltpu.VMEM((2, SEL_SLOTS, CMP_BLOCK, HEAD_DIM), F32),
                pltpu.SemaphoreType.DMA((2, 2)),
            ],
        ),
        out_shape=jax.ShapeDtypeStruct((s, N_HEADS, HEAD_DIM), F32),
        compiler_params=pltpu.CompilerParams(
            dimension_semantics=("arbitrary",), vmem_limit_bytes=VMEM_LIMIT_MID),
        name="decode_attn_b",
    )(page_table_flat, idx_flat, q, cache, new_rows, gates, ocw)


def nsa_decode_group(pn, row0, s, cache_nsa_kv, state_nsa_win, page_table, cmp_k, cmp_v):
    pages_per_seq = page_table.shape[1]
    past = pages_per_seq * PAGE_SIZE
    nb_dec = past // CMP_BLOCK + 1
    proj = Q_DIM + 6 * KV_DIM
    cos_s, sin_s = rope_tables(jnp.full((s,), past, jnp.int32))
    q_s, rows_s, win_s = nsa_layout(pn, cos_s, sin_s, n_seq=1, seq_len=s, row0=row0, tq=s)
    q_s = q_s[0].transpose(1, 0, 2)
    new_rows = jnp.concatenate([rows_s[0, 0], win_s[0]], axis=0).transpose(2, 0, 1, 3)
    new16 = jnp.repeat(new_rows, HPG, axis=2)
    pt_flat = page_table.reshape(-1)
    cache5 = cache_nsa_kv.reshape(cache_nsa_kv.shape[0], 4, PAGE_ROWS, CMP_BLOCK, HEAD_DIM)

    def summaries(comp, pe, w1, w2):
        past_c = paged_compress(pt_flat, cache5, pe, w1, w2, comp=comp)
        past_c = past_c.reshape(s, pages_per_seq, N_KV_HEADS, BLOCKS_PER_PAGE, HEAD_DIM)
        past_c = past_c.transpose(0, 2, 1, 3, 4).reshape(s, N_KV_HEADS, pages_per_seq * BLOCKS_PER_PAGE, HEAD_DIM)
        new_blk = jnp.pad(new_rows[:, comp].reshape(s * N_KV_HEADS, 1, HEAD_DIM), ((0, 0), (0, CMP_BLOCK - 1), (0, 0)))
        new_c = compress_blocks(new_blk, pe, w1, w2, tr=s * N_KV_HEADS, blocks=[0])
        both = jnp.concatenate([past_c, new_c.reshape(s, N_KV_HEADS, 1, HEAD_DIM)], axis=2)
        return jnp.pad(both, ((0, 0), (0, 0), (0, DEC_NB_PAD - nb_dec), (0, 0)))

    kc_s = summaries(0, *cmp_k)
    vc_s = summaries(1, *cmp_v)
    gates_s = pn[row0:row0 + s, proj:proj + 3 * N_HEADS].reshape(s, N_HEADS, 3)
    ocw, idx, win_out = decode_attn_a(q_s, kc_s, vc_s, new16, gates_s, state_nsa_win, nb=nb_dec, pos=past)
    idx_flat = idx[:, :, :N_KV_HEADS].transpose(0, 2, 1).reshape(-1)
    o_s = decode_attn_b(pt_flat, idx_flat, q_s, cache5, new16, gates_s, ocw, pages_per_seq=pages_per_seq, pos=past)
    return o_s.reshape(s, Q_DIM), new_rows[:, None, :4, :, None, :], win_out


def _matmul_residual_kernel(a_ref, a2_ref, w_ref, x_ref, o_ref, *, first_tiles):
    i = pl.program_id(0)

    @pl.when(i < first_tiles)
    def _():
        o_ref[...] = x_ref[...] + jnp.dot(a_ref[...].astype(BF16), w_ref[...], preferred_element_type=F32)

    @pl.when(i >= first_tiles)
    def _():
        o_ref[...] = x_ref[...] + jnp.dot(a2_ref[...].astype(BF16), w_ref[...], preferred_element_type=F32)


def matmul_residual(a, a2, w, x, *, tm):
    n1, k = a.shape
    n2 = a2.shape[0]
    n, d = x.shape
    assert n == n1 + n2 and n1 % tm == 0 and n2 % tm == 0
    first = n1 // tm
    return pl.pallas_call(
        functools.partial(_matmul_residual_kernel, first_tiles=first),
        grid=(n // tm,),
        in_specs=[
            pl.BlockSpec((tm, k), lambda i: (jnp.minimum(i, first - 1), 0)),
            pl.BlockSpec((tm, k), lambda i: (jnp.maximum(i - first, 0), 0)),
            pl.BlockSpec((k, d), lambda i: (0, 0)),
            pl.BlockSpec((tm, d), lambda i: (i, 0)),
        ],
        out_specs=pl.BlockSpec((tm, d), lambda i: (i, 0)),
        out_shape=jax.ShapeDtypeStruct((n, d), F32),
        compiler_params=pltpu.CompilerParams(
            dimension_semantics=("parallel",), vmem_limit_bytes=VMEM_LIMIT_MID),
        name="matmul_residual",
    )(a, a2, w, x)


def _rmsnorm_kernel(x_ref, g_ref, o_ref):
    x = x_ref[...]
    ms = jnp.mean(x * x, axis=-1, keepdims=True)
    o_ref[...] = x * lax.rsqrt(ms + RMS_EPS) * g_ref[...]


def rmsnorm_rows(x, g, *, row0, n_rows, tm):
    d = x.shape[1]
    assert row0 % tm == 0 and n_rows % tm == 0
    blk0 = row0 // tm
    return pl.pallas_call(
        _rmsnorm_kernel,
        grid=(n_rows // tm,),
        in_specs=[
            pl.BlockSpec((tm, d), lambda i: (blk0 + i, 0)),
            pl.BlockSpec((1, d), lambda i: (0, 0)),
        ],
        out_specs=pl.BlockSpec((tm, d), lambda i: (i, 0)),
        out_shape=jax.ShapeDtypeStruct((n_rows, d), F32),
        compiler_params=pltpu.CompilerParams(dimension_semantics=("parallel",)),
        name="rmsnorm_rows",
    )(x, g.reshape(1, d))


def kernel(x_prompt, x_sample, state_conv, cache_nsa_kv, state_nsa_win, page_table, norm_mix, norm_ffn, norm_final, conv_w_in, conv_w, conv_w_out, nsa_w_in, nsa_w_out, nsa_cmp_pe_k, nsa_cmp_w1_k, nsa_cmp_w2_k, nsa_cmp_pe_v, nsa_cmp_w1_v, nsa_cmp_w2_v, peer_w_q, peer_sub_keys, peer_u, peer_v):
    b, t, d = x_prompt.shape
    s = x_sample.shape[0]
    assert x_sample.shape[1] == 1 and d == D_MODEL
    n_prompt = b * t
    n_dec = DEC_ROWS

    u_all = peer_u.astype(BF16)
    vt_all = peer_v.astype(BF16).transpose(0, 2, 1)

    def peer(x, i):
        return peer_layer(
            x, norm_ffn[i], peer_w_q[i].astype(BF16),
            peer_sub_keys[i].reshape(2 * PEER_HEADS, PEER_KEYS, PEER_HALF_DIM).astype(BF16),
            u_all, vt_all, i)

    x = jnp.concatenate([x_prompt.reshape(n_prompt, d), x_sample.reshape(s, d),
                         jnp.zeros((n_dec - s, d), F32)], axis=0)

    p = rms_matmul_wide(x, norm_mix[0], conv_w_in[0].astype(BF16), tm=512, tn=3 * d // 2)
    st = state_conv[:, 0]
    pad_dec = lambda a: jnp.pad(a, ((0, n_dec - s), (0, 0)))
    x, tail, u_dec = conv_core(p, x, pad_dec(st[:, 0]), pad_dec(st[:, 1]), conv_w[0],
                               conv_w_out[0].astype(BF16), n_prompt=n_prompt, seq_len=t)
    tail_p = tail[:n_prompt // CONV_TM].reshape(b, t // CONV_TM, CONV_TAIL, d)
    conv_state_prompt = tail_p[:, -1, CONV_TAIL - 2:][:, None]
    conv_state_sample = jnp.stack([st[:, 1], u_dec[:s]], axis=1)[:, None]
    x = peer(x, 0)

    proj = Q_DIM + 6 * KV_DIM
    n_gate = 3 * N_HEADS
    w_in = jnp.pad(nsa_w_in[0], ((0, 0), (0, NSA_PROJ_PAD - proj - n_gate))).astype(BF16)
    pn = rms_matmul_wide(x, norm_mix[1], w_in, tm=512, tn=NSA_PROJ_PAD // 2)
    gates = pn[:, proj:proj + n_gate]
    w1_shape = (CMP_PASSES, CMP_PAIR * HEAD_DIM, CMP_HIDDEN)
    w1k, w2k = nsa_cmp_w1_k[0].astype(BF16).reshape(w1_shape), nsa_cmp_w2_k[0].astype(BF16)
    w1v, w2v = nsa_cmp_w1_v[0].astype(BF16).reshape(w1_shape), nsa_cmp_w2_v[0].astype(BF16)
    pe_k, pe_v = nsa_cmp_pe_k[0], nsa_cmp_pe_v[0]

    cos_p, sin_p = rope_tables(jnp.arange(t))
    q_p, rows_p, win_p = nsa_layout(pn, cos_p, sin_p, n_seq=b, seq_len=t, row0=0, tq=ATT_TQ)
    nb = t // CMP_BLOCK
    flat = rows_p.reshape(b * 4 * N_KV_HEADS * nb, CMP_BLOCK, HEAD_DIM)
    tr = N_KV_HEADS * nb
    kc_p = compress_blocks(flat, pe_k, w1k, w2k, tr=tr, blocks=[4 * i for i in range(b)])
    vc_p = compress_blocks(flat, pe_v, w1v, w2v, tr=tr, blocks=[4 * i + 1 for i in range(b)])
    gates_p = gates[:n_prompt].reshape(b, t, N_KV_HEADS, 3 * HPG).transpose(0, 2, 1, 3)
    o_p = nsa_prompt_attn(q_p, kc_p.reshape(b, N_KV_HEADS, nb, HEAD_DIM), vc_p.reshape(b, N_KV_HEADS, nb, HEAD_DIM),
                          rows_p, win_p, gates_p)
    wb_p = min(WINDOW, t)
    nsa_win_prompt = win_p[:, None, :, :, t - wb_p:, :]

    o_s, nsa_rows_sample, nsa_win_sample = nsa_decode_group(
        pn, n_prompt, s, cache_nsa_kv, state_nsa_win, page_table, (pe_k, w1k, w2k), (pe_v, w1v, w2v))
    o_s = jnp.pad(o_s, ((0, n_dec - s), (0, 0)))

    x = matmul_residual(o_p, o_s, nsa_w_out[0].astype(BF16), x, tm=256)
    x = peer(x, 1)

    y_prompt = rmsnorm_rows(x, norm_final, row0=0, n_rows=n_prompt, tm=512).reshape(b, t, d)
    y_sample = rmsnorm_rows(x, norm_final, row0=n_prompt, n_rows=s, tm=s).reshape(s, 1, d)
    return (y_prompt, y_sample, conv_state_prompt, conv_state_sample,
            rows_p, nsa_rows_sample, nsa_win_prompt, nsa_win_sample)
```
